```python
import jax, jax.numpy as jnp
from jax import lax
import numpy as np

D_MODEL = 2048
BATCH = 8
SEQ = 2048
DEPTH = 2

MEM_LEN = 256
N_GROUPS = 4
GROUP_WIDTH = D_MODEL // N_GROUPS
HEAD_DIM = 128
N_HEADS = GROUP_WIDTH // HEAD_DIM
ATTN_Q_BLOCK = 128
MLA_Q_RANK = 512
MLA_KV_RANK = 256
MLA_NOPE_DIM = 128
MLA_ROPE_DIM = 64
MLA_V_DIM = 128
ROPE_THETA = 10000.0
GMLP_CHUNK = 128
GMLP_GROUPS = 4
GMLP_GROUP_CH = GROUP_WIDTH // GMLP_GROUPS
MOBA_BLOCK = 256
MOBA_TOPK = 3
MOBA_Q_CHUNK = 32
X_HEADS = 4
X_HEAD_DIM = 128
X_WIDTH = X_HEADS * X_HEAD_DIM
D_FF = -(-(8 * D_MODEL) // (3 * 256)) * 256
EPS = 1e-6
NEG = -1e30

IN_SPLITS = (GROUP_WIDTH, GROUP_WIDTH, GROUP_WIDTH, N_HEADS,
             MLA_Q_RANK, MLA_KV_RANK, MLA_ROPE_DIM,
             GROUP_WIDTH, GROUP_WIDTH,
             GROUP_WIDTH, GROUP_WIDTH, GROUP_WIDTH)
IN_COLS = sum(IN_SPLITS)

kernel_name = 'hybrid_parallel_heads_fox_mla_gmlp_moba'


def rmsnorm(x, g):
    xf = x.astype(jnp.float32)
    y = xf * lax.rsqrt(jnp.mean(xf * xf, axis=-1, keepdims=True) + EPS)
    return (y * g.astype(jnp.float32)).astype(x.dtype)


def layernorm(x, g, b):
    xf = x.astype(jnp.float32)
    mu = jnp.mean(xf, axis=-1, keepdims=True)
    var = jnp.mean(jnp.square(xf - mu), axis=-1, keepdims=True)
    y = (xf - mu) * lax.rsqrt(var + EPS)
    return (y * g.astype(jnp.float32) + b.astype(jnp.float32)).astype(x.dtype)


def split_heads(t, h):
    b, s, _ = t.shape
    return t.reshape(b, s, h, -1).transpose(0, 2, 1, 3)


def merge_heads(t):
    b, h, s, d = t.shape
    return t.transpose(0, 2, 1, 3).reshape(b, s, h * d)


def causal_block_attention(q, k, v, decay=None):
    b, h, s, dq = q.shape
    nblk = s // ATTN_Q_BLOCK
    scale = dq ** -0.5
    kpos = jnp.arange(s)
    qb = q.reshape(b, h, nblk, ATTN_Q_BLOCK, dq).transpose(2, 0, 1, 3, 4)
    xs = (jnp.arange(nblk), qb)
    if decay is not None:
        xs = xs + (decay.reshape(b, h, nblk, ATTN_Q_BLOCK).transpose(2, 0, 1, 3),)

    def step(args):
        i, q_i = args[0], args[1]
        sc = jnp.einsum('bhqd,bhkd->bhqk', q_i, k).astype(jnp.float32) * scale
        if decay is not None:
            sc = sc + args[2][..., :, None] - decay[:, :, None, :]
        qpos = i * ATTN_Q_BLOCK + jnp.arange(ATTN_Q_BLOCK)
        sc = jnp.where(kpos[None, :] <= qpos[:, None], sc, NEG)
        p = jax.nn.softmax(sc, axis=-1).astype(v.dtype)
        return jnp.einsum('bhqk,bhkd->bhqd', p, v)

    o = lax.map(step, xs)
    return o.transpose(1, 2, 0, 3, 4).reshape(b, h, s, v.shape[-1])


def fox_mixer(q, k, v, f_logit, b_f, q_g, k_g):
    q = rmsnorm(split_heads(q, N_HEADS), q_g)
    k = rmsnorm(split_heads(k, N_HEADS), k_g)
    v = split_heads(v, N_HEADS)
    log_f = jax.nn.log_sigmoid(f_logit.astype(jnp.float32) + b_f.astype(jnp.float32))
    decay = jnp.cumsum(log_f, axis=1).transpose(0, 2, 1)
    return merge_heads(causal_block_attention(q, k, v, decay))


def apply_rope(x, cos, sin):
    half = x.shape[-1] // 2
    x1, x2 = x[..., :half], x[..., half:]
    return jnp.concatenate([x1 * cos - x2 * sin, x1 * sin + x2 * cos], axis=-1).astype(x.dtype)


def mla_mixer(c_q, c_kv, k_rope, q_lora_g, w_uq, kv_lora_g, w_ukv, q_g, k_g):
    b, s, _ = c_q.shape
    q = split_heads(rmsnorm(c_q, q_lora_g) @ w_uq, N_HEADS)
    kv = split_heads(rmsnorm(c_kv, kv_lora_g) @ w_ukv, N_HEADS)
    q_nope = rmsnorm(q[..., :MLA_NOPE_DIM], q_g[:MLA_NOPE_DIM])
    q_pe = rmsnorm(q[..., MLA_NOPE_DIM:], q_g[MLA_NOPE_DIM:])
    k_nope = rmsnorm(kv[..., :MLA_NOPE_DIM], k_g[:MLA_NOPE_DIM])
    v = kv[..., MLA_NOPE_DIM:]
    k_pe = rmsnorm(k_rope, k_g[MLA_NOPE_DIM:])[:, None]
    pos = jnp.arange(s, dtype=jnp.float32)
    inv_freq = ROPE_THETA ** (-jnp.arange(0, MLA_ROPE_DIM, 2, dtype=jnp.float32) / MLA_ROPE_DIM)
    ang = pos[:, None] * inv_freq[None, :]
    cos, sin = jnp.cos(ang), jnp.sin(ang)
    q_pe = apply_rope(q_pe, cos, sin)
    k_pe = apply_rope(k_pe, cos, sin)
    q = jnp.concatenate([q_nope, q_pe], axis=-1)
    k = jnp.concatenate([k_nope, jnp.broadcast_to(k_pe, (b, N_HEADS, s, MLA_ROPE_DIM))], axis=-1)
    return merge_heads(causal_block_attention(q, k, v))


def gmlp_mixer(u, v, ln_g, ln_b, w_s, b_s):
    b, s, _ = v.shape
    u = jax.nn.gelu(u)
    v = layernorm(jax.nn.gelu(v), ln_g, ln_b)
    vc = v.reshape(b, s // GMLP_CHUNK, GMLP_CHUNK, GMLP_GROUPS, GMLP_GROUP_CH)
    w_causal = w_s * jnp.tril(jnp.ones((GMLP_CHUNK, GMLP_CHUNK), w_s.dtype))
    mixed = jnp.einsum('gts,bnsgc->bntgc', w_causal, vc) + b_s.T[None, None, :, :, None]
    return u * mixed.reshape(b, s, GROUP_WIDTH)


def moba_mixer(q, k, v, q_g, k_g):
    b, s, _ = q.shape
    q = rmsnorm(split_heads(q, N_HEADS), q_g)
    k = rmsnorm(split_heads(k, N_HEADS), k_g)
    v = split_heads(v, N_HEADS)
    nb = -(-s // MOBA_BLOCK)
    s_pad = nb * MOBA_BLOCK
    pad = ((0, 0), (0, 0), (0, s_pad - s), (0, 0))
    q, k, v = jnp.pad(q, pad), jnp.pad(k, pad), jnp.pad(v, pad)
    kb = k.reshape(b, N_HEADS, nb, MOBA_BLOCK, HEAD_DIM)
    vb = v.reshape(b, N_HEADS, nb, MOBA_BLOCK, HEAD_DIM)
    k_mean = jnp.mean(kb, axis=3)
    gate = jnp.einsum('bhsd,bhnd->bhsn', q, k_mean).astype(jnp.float32)
    q_block = jnp.arange(s_pad) // MOBA_BLOCK
    gate = jnp.where(jnp.arange(nb)[None, :] < q_block[:, None], gate, NEG)
    top = min(MOBA_TOPK, nb)
    _, sel = lax.top_k(gate, top)
    sel_ok = jnp.arange(top)[None, :] < q_block[:, None]
    nqc = s_pad // MOBA_Q_CHUNK
    scale = HEAD_DIM ** -0.5
    gather = jax.vmap(jax.vmap(lambda blocks, idx: blocks[idx]))

    def step(args):
        i, q_i, sel_i, ok_i = args
        k_sel = gather(kb, sel_i)
        v_sel = gather(vb, sel_i)
        s_sel = jnp.einsum('bhqd,bhqjkd->bhqjk', q_i, k_sel).astype(jnp.float32) * scale
        s_sel = jnp.where(ok_i[None, None, :, :, None], s_sel, NEG)
        blk = (i * MOBA_Q_CHUNK) // MOBA_BLOCK
        k_own = lax.dynamic_index_in_dim(kb, blk, axis=2, keepdims=False)
        v_own = lax.dynamic_index_in_dim(vb, blk, axis=2, keepdims=False)
        s_own = jnp.einsum('bhqd,bhkd->bhqk', q_i, k_own).astype(jnp.float32) * scale
        qpos = i * MOBA_Q_CHUNK + jnp.arange(MOBA_Q_CHUNK)
        kpos = blk * MOBA_BLOCK + jnp.arange(MOBA_BLOCK)
        s_own = jnp.where(kpos[None, :] <= qpos[:, None], s_own, NEG)
        s_all = jnp.concatenate([s_sel.reshape(b, N_HEADS, MOBA_Q_CHUNK, top * MOBA_BLOCK), s_own], axis=-1)
        p = jax.nn.softmax(s_all, axis=-1).astype(v.dtype)
        p_sel = p[..., :top * MOBA_BLOCK].reshape(b, N_HEADS, MOBA_Q_CHUNK, top, MOBA_BLOCK)
        p_own = p[..., top * MOBA_BLOCK:]
        return (jnp.einsum('bhqjk,bhqjkd->bhqd', p_sel, v_sel)
                + jnp.einsum('bhqk,bhkd->bhqd', p_own, v_own))

    xs = (jnp.arange(nqc),
          q.reshape(b, N_HEADS, nqc, MOBA_Q_CHUNK, HEAD_DIM).transpose(2, 0, 1, 3, 4),
          sel.reshape(b, N_HEADS, nqc, MOBA_Q_CHUNK, top).transpose(2, 0, 1, 3, 4),
          sel_ok.reshape(nqc, MOBA_Q_CHUNK, top))
    o = lax.map(step, xs)
    o = o.transpose(1, 2, 0, 3, 4).reshape(b, N_HEADS, s_pad, HEAD_DIM)[:, :, :s]
    return merge_heads(o)


def memory_cross_attention(h, mem, mem_g, w_xq, w_xkv, q_g, k_g, w_xo):
    q = rmsnorm(split_heads(h @ w_xq, X_HEADS), q_g)
    kv = rmsnorm(mem, mem_g) @ w_xkv
    k = rmsnorm(split_heads(kv[..., :X_WIDTH], X_HEADS), k_g)
    v = split_heads(kv[..., X_WIDTH:], X_HEADS)
    sc = jnp.einsum('bhqd,bhmd->bhqm', q, k).astype(jnp.float32) * (X_HEAD_DIM ** -0.5)
    p = jax.nn.softmax(sc, axis=-1).astype(v.dtype)
    return merge_heads(jnp.einsum('bhqm,bhmd->bhqd', p, v)) @ w_xo


def swiglu(h, w_gate_up, w_down):
    gu = h @ w_gate_up
    return (jax.nn.silu(gu[..., :D_FF]) * gu[..., D_FF:]) @ w_down


def setup_inputs(seed: int = 0) -> dict:
    key = jax.random.key(seed)
    ks = iter(jax.random.split(key, 40))
    L = DEPTH
    f32 = jnp.float32

    def w(shape, fan_in):
        return jax.random.normal(next(ks), shape, f32) * fan_in ** -0.5

    def gain(shape):
        return 1.0 + 0.02 * jax.random.normal(next(ks), shape, f32)

    def small(shape):
        return 0.02 * jax.random.normal(next(ks), shape, f32)

    mla_qk = MLA_NOPE_DIM + MLA_ROPE_DIM
    return {
        'x': jax.random.normal(next(ks), (BATCH, SEQ, D_MODEL), f32),
        'mem': jax.random.normal(next(ks), (BATCH, MEM_LEN, D_MODEL), f32),
        'mix_norm': gain((L, D_MODEL)),
        'w_in': w((L, D_MODEL, IN_COLS), D_MODEL),
        'fox_b_f': small((L, N_HEADS)),
        'fox_q_norm': gain((L, HEAD_DIM)),
        'fox_k_norm': gain((L, HEAD_DIM)),
        'mla_q_lora_norm': gain((L, MLA_Q_RANK)),
        'mla_w_uq': w((L, MLA_Q_RANK, N_HEADS * mla_qk), MLA_Q_RANK),
        'mla_kv_lora_norm': gain((L, MLA_KV_RANK)),
        'mla_w_ukv': w((L, MLA_KV_RANK, N_HEADS * (MLA_NOPE_DIM + MLA_V_DIM)), MLA_KV_RANK),
        'mla_q_norm': gain((L, mla_qk)),
        'mla_k_norm': gain((L, mla_qk)),
        'gmlp_ln_g': gain((L, GROUP_WIDTH)),
        'gmlp_ln_b': small((L, GROUP_WIDTH)),
        'gmlp_w_s': w((L, GMLP_GROUPS, GMLP_CHUNK, GMLP_CHUNK), GMLP_CHUNK),
        'gmlp_b_s': gain((L, GMLP_GROUPS, GMLP_CHUNK)),
        'moba_q_norm': gain((L, HEAD_DIM)),
        'moba_k_norm': gain((L, HEAD_DIM)),
        'group_norm': gain((L, N_GROUPS, GROUP_WIDTH)),
        'w_out': w((L, N_GROUPS * GROUP_WIDTH, D_MODEL), N_GROUPS * GROUP_WIDTH),
        'xattn_norm': gain((L, D_MODEL)),
        'mem_norm': gain((L, D_MODEL)),
        'w_xq': w((L, D_MODEL, X_WIDTH), D_MODEL),
        'w_xkv': w((L, D_MODEL, 2 * X_WIDTH), D_MODEL),
        'xattn_q_norm': gain((L, X_HEAD_DIM)),
        'xattn_k_norm': gain((L, X_HEAD_DIM)),
        'w_xo': w((L, X_WIDTH, D_MODEL), X_WIDTH),
        'ffn_norm': gain((L, D_MODEL)),
        'w_gate_up': w((L, D_MODEL, 2 * D_FF), D_MODEL),
        'w_down': w((L, D_FF, D_MODEL), D_FF),
    }


def reference(x, mem, mix_norm, w_in, fox_b_f, fox_q_norm, fox_k_norm,
              mla_q_lora_norm, mla_w_uq, mla_kv_lora_norm, mla_w_ukv, mla_q_norm, mla_k_norm,
              gmlp_ln_g, gmlp_ln_b, gmlp_w_s, gmlp_b_s, moba_q_norm, moba_k_norm,
              group_norm, w_out, xattn_norm, mem_norm, w_xq, w_xkv, xattn_q_norm, xattn_k_norm,
              w_xo, ffn_norm, w_gate_up, w_down):
    split_idx = np.cumsum(IN_SPLITS)[:-1].tolist()
    for l in range(DEPTH):
        proj = rmsnorm(x, mix_norm[l]) @ w_in[l]
        (fq, fk, fv, ff, mcq, mckv, mkr, gu, gv, bq, bk, bv) = jnp.split(proj, split_idx, axis=-1)
        o_a = fox_mixer(fq, fk, fv, ff, fox_b_f[l], fox_q_norm[l], fox_k_norm[l])
        o_b = mla_mixer(mcq, mckv, mkr, mla_q_lora_norm[l], mla_w_uq[l], mla_kv_lora_norm[l],
                        mla_w_ukv[l], mla_q_norm[l], mla_k_norm[l])
        o_c = gmlp_mixer(gu, gv, gmlp_ln_g[l], gmlp_ln_b[l], gmlp_w_s[l], gmlp_b_s[l])
        o_d = moba_mixer(bq, bk, bv, moba_q_norm[l], moba_k_norm[l])
        mixed = jnp.concatenate([rmsnorm(o_a, group_norm[l, 0]), rmsnorm(o_b, group_norm[l, 1]),
                                 rmsnorm(o_c, group_norm[l, 2]), rmsnorm(o_d, group_norm[l, 3])], axis=-1)
        x = x + mixed @ w_out[l]
        x = x + memory_cross_attention(rmsnorm(x, xattn_norm[l]), mem, mem_norm[l], w_xq[l], w_xkv[l],
                                       xattn_q_norm[l], xattn_k_norm[l], w_xo[l])
        x = x + swiglu(rmsnorm(x, ffn_norm[l]), w_gate_up[l], w_down[l])
    return x
```

```python
import functools

import jax
import jax.numpy as jnp
from jax import lax
from jax.experimental import pallas as pl
from jax.experimental.pallas import tpu as pltpu

F32 = jnp.float32
BF16 = jnp.bfloat16

D_MODEL = 2048
BATCH = 8
SEQ = 2048
DEPTH = 2
TOKENS = BATCH * SEQ
MEM_LEN = 256
GROUP_WIDTH = 512
HEAD_DIM = 128
N_HEADS = 4
MLA_Q_RANK = 512
MLA_KV_RANK = 256
MLA_NOPE_DIM = 128
MLA_ROPE_DIM = 64
MLA_QK_DIM = MLA_NOPE_DIM + MLA_ROPE_DIM
ROPE_THETA = 10000.0
GMLP_CHUNK = 128
MOBA_BLOCK = 256
MOBA_TOPK = 3
N_MOBA_BLOCKS = SEQ // MOBA_BLOCK
D_FF = 5632
EPS = 1e-6
NEG = -1e30

V7X_VMEM_LIMIT_BYTES = 56 * 1024 * 1024

COL_FQ, COL_FK, COL_FV = 0, 512, 1024
COL_MCQ = 1536
COL_GU, COL_GV = 2048, 2560
COL_BQ, COL_BK, COL_BV = 3072, 3584, 4096
COL_MCKV = 4608
COL_KROPE = 4864
PROJ_COLS = 5120
SMALL_COLS = 128

TQ = 256
TK = 256
NQ = SEQ // TQ
NKV = SEQ // TK
INPROJ_BM, INPROJ_BN = 1024, 1024
ROW_TILE = 512
FFN_BM, FFN_BF = 512, 512


def _cparams(sem):
    return pltpu.CompilerParams(dimension_semantics=sem,
                                vmem_limit_bytes=V7X_VMEM_LIMIT_BYTES)


def _rms(x):
    return x * lax.rsqrt(jnp.mean(x * x, axis=-1, keepdims=True) + EPS)


def _dot(a, b):
    return jnp.dot(a, b, preferred_element_type=F32)


def _dot_nt(a, b, precision=None):
    return lax.dot_general(a, b, (((1,), (1,)), ((), ())),
                           preferred_element_type=F32, precision=precision)


def _softmax_step(s, v, m, l, acc):
    m_new = jnp.maximum(m, jnp.max(s, axis=-1, keepdims=True))
    alpha = jnp.exp(m - m_new)
    p = jnp.exp(s - m_new)
    l = alpha * l + jnp.sum(p, axis=-1, keepdims=True)
    acc = alpha * acc + _dot(p.astype(BF16), v)
    return m_new, l, acc


def _causal_mask(s):
    r = lax.broadcasted_iota(jnp.int32, s.shape, 0)
    c = lax.broadcasted_iota(jnp.int32, s.shape, 1)
    return jnp.where(c <= r, s, NEG)


def _softmax_init(d):
    return (jnp.full((TQ, 1), NEG, F32), jnp.zeros((TQ, 1), F32), jnp.zeros((TQ, d), F32))


def _group_out(heads, gn_ref, o_ref):
    o = jnp.concatenate(heads, axis=-1)
    o_ref[...] = (_rms(o) * gn_ref[...]).astype(o_ref.dtype)


def _inproj_kernel(x_ref, g_ref, w_ref, ws_ref, o_ref, os_ref, h_ref):
    @pl.when(pl.program_id(1) == 0)
    def _():
        h = (_rms(x_ref[...]) * g_ref[...]).astype(BF16)
        h_ref[...] = h
        os_ref[...] = _dot(h, ws_ref[...])

    o_ref[...] = _dot(h_ref[...], w_ref[...]).astype(o_ref.dtype)


def _inproj(x, g, w, ws):
    bm, bn = INPROJ_BM, INPROJ_BN
    return pl.pallas_call(
        _inproj_kernel,
        grid=(TOKENS // bm, PROJ_COLS // bn),
        in_specs=[pl.BlockSpec((bm, D_MODEL), lambda m, n: (m, 0)),
                  pl.BlockSpec((1, D_MODEL), lambda m, n: (0, 0)),
                  pl.BlockSpec((D_MODEL, bn), lambda m, n: (0, n)),
                  pl.BlockSpec((D_MODEL, SMALL_COLS), lambda m, n: (0, 0))],
        out_specs=[pl.BlockSpec((bm, bn), lambda m, n: (m, n)),
                   pl.BlockSpec((bm, SMALL_COLS), lambda m, n: (m, 0))],
        out_shape=[jax.ShapeDtypeStruct((TOKENS, PROJ_COLS), BF16),
                   jax.ShapeDtypeStruct((TOKENS, SMALL_COLS), F32)],
        scratch_shapes=[pltpu.VMEM((bm, D_MODEL), BF16)],
        compiler_params=_cparams(("arbitrary", "arbitrary")),
        name="inproj",
    )(x, g, w, ws)


def _fox_kernel(q_ref, k_ref, v_ref, f_ref, bf_ref, qg_ref, kg_ref, gn_ref, o_ref,
                kn_ref, c_ref):
    i = pl.program_id(1)
    scale = HEAD_DIM ** -0.5

    @pl.when(i == 0)
    def _prep():
        for h in range(N_HEADS):
            sl = slice(h * HEAD_DIM, (h + 1) * HEAD_DIM)
            kn_ref[:, sl] = (_rms(k_ref[:, sl].astype(F32)) * kg_ref[...]).astype(BF16)
        x = f_ref[...] + bf_ref[...]
        c = jnp.minimum(x, 0.0) - jnp.log1p(jnp.exp(-jnp.abs(x)))
        row = lax.broadcasted_iota(jnp.int32, c.shape, 0)
        step = 1
        while step < SEQ:
            c = c + jnp.where(row >= step, pltpu.roll(c, step, 0), 0.0)
            step *= 2
        ct = c.T
        for j in range(NKV):
            c_ref[j] = ct[0:8, j * TK:(j + 1) * TK]

    heads = []
    for h in range(N_HEADS):
        sl = slice(h * HEAD_DIM, (h + 1) * HEAD_DIM)
        qn = (_rms(q_ref[:, sl].astype(F32)) * qg_ref[...] * scale).astype(BF16)

        def logits(j, sl=sl, qn=qn, h=h):
            off = pl.multiple_of(j * TK, TK)
            s = _dot_nt(qn, kn_ref[pl.ds(off, TK), sl])
            return s - c_ref[j][h:h + 1, :], v_ref[pl.ds(off, TK), sl]

        def body(j, carry, logits=logits):
            s, v = logits(j)
            return _softmax_step(s, v, *carry)

        carry = lax.fori_loop(0, i, body, _softmax_init(HEAD_DIM))
        s, v = logits(i)
        m, l, acc = _softmax_step(_causal_mask(s), v, *carry)
        heads.append(acc / l)
    _group_out(heads, gn_ref, o_ref)


def _fox(proj, small, b_f, qg, kg, gn):
    row = lambda w: pl.BlockSpec((1, w), lambda b, i: (0, 0))
    return pl.pallas_call(
        _fox_kernel,
        grid=(BATCH, NQ),
        in_specs=[pl.BlockSpec((TQ, GROUP_WIDTH), lambda b, i: (b * NQ + i, COL_FQ // GROUP_WIDTH)),
                  pl.BlockSpec((SEQ, GROUP_WIDTH), lambda b, i: (b, COL_FK // GROUP_WIDTH)),
                  pl.BlockSpec((SEQ, GROUP_WIDTH), lambda b, i: (b, COL_FV // GROUP_WIDTH)),
                  pl.BlockSpec((SEQ, SMALL_COLS), lambda b, i: (b, 0)),
                  row(SMALL_COLS), row(HEAD_DIM), row(HEAD_DIM), row(GROUP_WIDTH)],
        out_specs=pl.BlockSpec((TQ, GROUP_WIDTH), lambda b, i: (b * NQ + i, 0)),
        out_shape=jax.ShapeDtypeStruct((TOKENS, GROUP_WIDTH), BF16),
        scratch_shapes=[pltpu.VMEM((SEQ, GROUP_WIDTH), BF16),
                        pltpu.VMEM((NKV, 8, TK), F32)],
        compiler_params=_cparams(("arbitrary", "arbitrary")),
        name="fox",
    )(proj, proj, proj, small, b_f, qg, kg, gn)


MLA_Q_HEAD_COLS = 384
MLA_K_COLS = 256


def _rope128(a, b, ga, gb, cos, sin):
    r = lax.rsqrt(jnp.sum(a * a, axis=-1, keepdims=True) * (1.0 / (2 * MLA_ROPE_DIM)) + EPS)
    return r * (a * ga * cos + b * gb * sin)


def _mla_kernel(cq_ref, ckv_ref, kr_ref, cos_ref, sin_ref, cosq_ref, sinq_ref,
                qlg_ref, wuq_ref, kvlg_ref, wukv_ref, qgn_ref, qga_ref, qgb_ref,
                kgn_ref, kga_ref, kgb_ref, gn_ref, o_ref, kf_ref, v_ref):
    i = pl.program_id(1)
    scale = MLA_QK_DIM ** -0.5
    chunk = 512

    @pl.when(i == 0)
    def _prep():
        for c in range(SEQ // chunk):
            rows = slice(c * chunk, (c + 1) * chunk)
            ckv = (_rms(ckv_ref[rows, :].astype(F32)) * kvlg_ref[...]).astype(BF16)
            kv = _dot(ckv, wukv_ref[...])
            kr = kr_ref[rows, :].astype(F32)
            kpe = _rope128(kr[:, :128], kr[:, 128:], kga_ref[...], kgb_ref[...],
                           cos_ref[rows, :], sin_ref[rows, :]).astype(BF16)
            for h in range(N_HEADS):
                sl = slice(h * HEAD_DIM, (h + 1) * HEAD_DIM)
                kf_ref[h, rows, 0:128] = (_rms(kv[:, sl]) * kgn_ref[...]).astype(BF16)
                kf_ref[h, rows, 128:256] = kpe
            v_ref[rows, :] = kv[:, GROUP_WIDTH:].astype(BF16)

    cq = (_rms(cq_ref[...].astype(F32)) * qlg_ref[...]).astype(BF16)
    qraw = _dot(cq, wuq_ref[...])
    heads = []
    for h in range(N_HEADS):
        base = h * MLA_Q_HEAD_COLS
        qn = _rms(qraw[:, base:base + 128]) * qgn_ref[...]
        qpe = _rope128(qraw[:, base + 128:base + 256], qraw[:, base + 256:base + 384],
                       qga_ref[...], qgb_ref[...], cosq_ref[...], sinq_ref[...])
        qf = (jnp.concatenate([qn, qpe], axis=-1) * scale).astype(BF16)
        sl = slice(h * HEAD_DIM, (h + 1) * HEAD_DIM)

        def logits(j, qf=qf, h=h, sl=sl):
            off = pl.multiple_of(j * TK, TK)
            return _dot_nt(qf, kf_ref[h, pl.ds(off, TK), :]), v_ref[pl.ds(off, TK), sl]

        def body(j, carry, logits=logits):
            s, v = logits(j)
            return _softmax_step(s, v, *carry)

        carry = lax.fori_loop(0, i, body, _softmax_init(HEAD_DIM))
        s, v = logits(i)
        m, l, acc = _softmax_step(_causal_mask(s), v, *carry)
        heads.append(acc / l)
    _group_out(heads, gn_ref, o_ref)


def _mla(proj, cos, sin, qlg, wuq, kvlg, wukv, qgn, qga, qgb, kgn, kga, kgb, gn):
    full = lambda a: pl.BlockSpec(a.shape, lambda b, i: (0,) * a.ndim)
    return pl.pallas_call(
        _mla_kernel,
        grid=(BATCH, NQ),
        in_specs=[pl.BlockSpec((TQ, MLA_Q_RANK), lambda b, i: (b * NQ + i, COL_MCQ // MLA_Q_RANK)),
                  pl.BlockSpec((SEQ, MLA_KV_RANK), lambda b, i: (b, COL_MCKV // MLA_KV_RANK)),
                  pl.BlockSpec((SEQ, 256), lambda b, i: (b, COL_KROPE // 256)),
                  full(cos), full(sin),
                  pl.BlockSpec((TQ, 128), lambda b, i: (i, 0)),
                  pl.BlockSpec((TQ, 128), lambda b, i: (i, 0)),
                  full(qlg), full(wuq), full(kvlg), full(wukv), full(qgn), full(qga), full(qgb),
                  full(kgn), full(kga), full(kgb), full(gn)],
        out_specs=pl.BlockSpec((TQ, GROUP_WIDTH), lambda b, i: (b * NQ + i, 0)),
        out_shape=jax.ShapeDtypeStruct((TOKENS, GROUP_WIDTH), BF16),
        scratch_shapes=[pltpu.VMEM((N_HEADS, SEQ, MLA_K_COLS), BF16),
                        pltpu.VMEM((SEQ, GROUP_WIDTH), BF16)],
        compiler_params=_cparams(("arbitrary", "arbitrary")),
        name="mla",
    )(proj, proj, proj, cos, sin, cos, sin, qlg, wuq, kvlg, wukv, qgn, qga, qgb, kgn, kga, kgb, gn)


def _gelu(x):
    return 0.5 * x * (1.0 + jnp.tanh(0.7978845608028654 * (x + 0.044715 * (x * x * x))))


def _gmlp_kernel(u_ref, v_ref, lg_ref, lb_ref, ws_ref, bs_ref, gn_ref, o_ref, mix_ref):
    v = _gelu(v_ref[...].astype(F32))
    mu = jnp.mean(v, axis=-1, keepdims=True)
    vc = v - mu
    var = jnp.mean(vc * vc, axis=-1, keepdims=True)
    vn = (vc * lax.rsqrt(var + EPS) * lg_ref[...] + lb_ref[...]).astype(BF16)
    r = lax.broadcasted_iota(jnp.int32, (GMLP_CHUNK, GMLP_CHUNK), 0)
    c = lax.broadcasted_iota(jnp.int32, (GMLP_CHUNK, GMLP_CHUNK), 1)
    for g in range(GROUP_WIDTH // HEAD_DIM):
        w = jnp.where(c <= r, ws_ref[g], 0.0).astype(BF16)
        cols = slice(g * HEAD_DIM, (g + 1) * HEAD_DIM)
        for n in range(ROW_TILE // GMLP_CHUNK):
            rows = slice(n * GMLP_CHUNK, (n + 1) * GMLP_CHUNK)
            mix_ref[rows, cols] = _dot(w, vn[rows, cols]) + bs_ref[:, cols]
    o = _gelu(u_ref[...].astype(F32)) * mix_ref[...]
    o_ref[...] = (_rms(o) * gn_ref[...]).astype(o_ref.dtype)


def _gmlp(proj, lg, lb, ws, bs, gn):
    full = lambda a: pl.BlockSpec(a.shape, lambda m: (0,) * a.ndim)
    return pl.pallas_call(
        _gmlp_kernel,
        grid=(TOKENS // ROW_TILE,),
        in_specs=[pl.BlockSpec((ROW_TILE, GROUP_WIDTH), lambda m: (m, COL_GU // GROUP_WIDTH)),
                  pl.BlockSpec((ROW_TILE, GROUP_WIDTH), lambda m: (m, COL_GV // GROUP_WIDTH)),
                  full(lg), full(lb), full(ws), full(bs), full(gn)],
        out_specs=pl.BlockSpec((ROW_TILE, GROUP_WIDTH), lambda m: (m, 0)),
        out_shape=jax.ShapeDtypeStruct((TOKENS, GROUP_WIDTH), BF16),
        scratch_shapes=[pltpu.VMEM((ROW_TILE, GROUP_WIDTH), F32)],
        compiler_params=_cparams(("arbitrary",)),
        name="gmlp",
    )(proj, proj, lg, lb, ws, bs, gn)


def _moba_kernel(q_ref, k_ref, v_ref, qg_ref, kg_ref, gn_ref, o_ref, kn_ref, km_ref):
    i = pl.program_id(1)
    scale = HEAD_DIM ** -0.5

    @pl.when(i == 0)
    def _prep():
        for h in range(N_HEADS):
            sl = slice(h * HEAD_DIM, (h + 1) * HEAD_DIM)
            means = []
            for n in range(N_MOBA_BLOCKS):
                rows = slice(n * MOBA_BLOCK, (n + 1) * MOBA_BLOCK)
                kn = _rms(k_ref[rows, sl].astype(F32)) * kg_ref[...]
                kn_ref[rows, sl] = kn.astype(BF16)
                means.append(jnp.mean(kn, axis=0, keepdims=True))
            km_ref[h] = jnp.concatenate(means, axis=0)

    blk = lax.broadcasted_iota(jnp.int32, (N_MOBA_BLOCKS, TQ), 0)
    eye = (lax.broadcasted_iota(jnp.int32, (TQ, TQ), 0)
           == lax.broadcasted_iota(jnp.int32, (TQ, TQ), 1)).astype(BF16)
    pick = lax.broadcasted_iota(jnp.int32, (HEAD_DIM, TK), 0)
    heads = []
    for h in range(N_HEADS):
        sl = slice(h * HEAD_DIM, (h + 1) * HEAD_DIM)
        q32 = _rms(q_ref[:, sl].astype(F32)) * qg_ref[...]
        qn = (q32 * scale).astype(BF16)

        gate = _dot_nt(km_ref[h], q32, precision=lax.Precision.HIGHEST)
        gate = jnp.where(blk < i, gate, NEG)
        sel_t = jnp.zeros((N_MOBA_BLOCKS, TQ), F32)
        for n in range(N_MOBA_BLOCKS):
            gn_row = gate[n:n + 1, :]
            beats = jnp.where(gate > gn_row, 1.0, jnp.where((gate == gn_row) & (blk < n), 1.0, 0.0))
            rank = jnp.sum(beats, axis=0, keepdims=True)
            past = jnp.full((1, TQ), n, jnp.int32) < i
            chosen = jnp.where(rank < MOBA_TOPK, jnp.where(past, 1.0, 0.0), 0.0)
            sel_t = jnp.where(blk == n, chosen, sel_t)
        sel_t = jnp.concatenate([sel_t, jnp.zeros((HEAD_DIM - N_MOBA_BLOCKS, TQ), F32)], axis=0)
        sel = _dot_nt(eye, sel_t.astype(BF16)).astype(BF16)

        def logits(j, qn=qn, sl=sl):
            off = pl.multiple_of(j * TK, TK)
            return _dot_nt(qn, kn_ref[pl.ds(off, TK), sl]), v_ref[pl.ds(off, TK), sl]

        s, v = logits(i)
        carry = _softmax_step(_causal_mask(s), v, *_softmax_init(HEAD_DIM))

        def body(j, carry, logits=logits, sel=sel):
            s, v = logits(j)
            keep = _dot(sel, (pick == j).astype(BF16))
            return _softmax_step(jnp.where(keep > 0.5, s, NEG), v, *carry)

        m, l, acc = lax.fori_loop(0, i, body, carry)
        heads.append(acc / l)
    _group_out(heads, gn_ref, o_ref)


def _moba(proj, qg, kg, gn):
    row = lambda w: pl.BlockSpec((1, w), lambda b, i: (0, 0))
    return pl.pallas_call(
        _moba_kernel,
        grid=(BATCH, NQ),
        in_specs=[pl.BlockSpec((TQ, GROUP_WIDTH), lambda b, i: (b * NQ + i, COL_BQ // GROUP_WIDTH)),
                  pl.BlockSpec((SEQ, GROUP_WIDTH), lambda b, i: (b, COL_BK // GROUP_WIDTH)),
                  pl.BlockSpec((SEQ, GROUP_WIDTH), lambda b, i: (b, COL_BV // GROUP_WIDTH)),
                  row(HEAD_DIM), row(HEAD_DIM), row(GROUP_WIDTH)],
        out_specs=pl.BlockSpec((TQ, GROUP_WIDTH), lambda b, i: (b * NQ + i, 0)),
        out_shape=jax.ShapeDtypeStruct((TOKENS, GROUP_WIDTH), BF16),
        scratch_shapes=[pltpu.VMEM((SEQ, GROUP_WIDTH), BF16),
                        pltpu.VMEM((N_HEADS, N_MOBA_BLOCKS, HEAD_DIM), F32)],
        compiler_params=_cparams(("arbitrary", "arbitrary")),
        name="moba",
    )(proj, proj, proj, qg, kg, gn)


def _outproj_kernel(x_ref, a_ref, b_ref, c_ref, d_ref, w_ref, o_ref):
    mixed = jnp.concatenate([a_ref[...], b_ref[...], c_ref[...], d_ref[...]], axis=-1)
    o_ref[...] = x_ref[...] + _dot(mixed, w_ref[...])


def _outproj(x, oa, ob, oc, od, w):
    grp = pl.BlockSpec((ROW_TILE, GROUP_WIDTH), lambda m: (m, 0))
    return pl.pallas_call(
        _outproj_kernel,
        grid=(TOKENS // ROW_TILE,),
        in_specs=[pl.BlockSpec((ROW_TILE, D_MODEL), lambda m: (m, 0)), grp, grp, grp, grp,
                  pl.BlockSpec((D_MODEL, D_MODEL), lambda m: (0, 0))],
        out_specs=pl.BlockSpec((ROW_TILE, D_MODEL), lambda m: (m, 0)),
        out_shape=jax.ShapeDtypeStruct((TOKENS, D_MODEL), F32),
        compiler_params=_cparams(("arbitrary",)),
        name="outproj",
    )(x, oa, ob, oc, od, w)


def _memkv_kernel(m_ref, g_ref, w_ref, kg_ref, k_ref, v_ref):
    h = (_rms(m_ref[...]) * g_ref[...]).astype(BF16)
    kv = _dot(h, w_ref[...])
    for hd in range(N_HEADS):
        sl = slice(hd * HEAD_DIM, (hd + 1) * HEAD_DIM)
        k_ref[:, sl] = (_rms(kv[:, sl]) * kg_ref[...]).astype(BF16)
    v_ref[...] = kv[:, GROUP_WIDTH:].astype(BF16)


def _memkv(mem, g, w, kg):
    out = pl.BlockSpec((MEM_LEN, GROUP_WIDTH), lambda b: (b, 0))
    return pl.pallas_call(
        _memkv_kernel,
        grid=(BATCH,),
        in_specs=[pl.BlockSpec((MEM_LEN, D_MODEL), lambda b: (b, 0)),
                  pl.BlockSpec((1, D_MODEL), lambda b: (0, 0)),
                  pl.BlockSpec((D_MODEL, 2 * GROUP_WIDTH), lambda b: (0, 0)),
                  pl.BlockSpec((1, HEAD_DIM), lambda b: (0, 0))],
        out_specs=[out, out],
        out_shape=[jax.ShapeDtypeStruct((BATCH * MEM_LEN, GROUP_WIDTH), BF16)] * 2,
        compiler_params=_cparams(("arbitrary",)),
        name="memkv",
    )(mem, g, w, kg)


def _xattn_kernel(x_ref, g_ref, wq_ref, qg_ref, k_ref, v_ref, wo_ref, o_ref):
    x = x_ref[...]
    h = (_rms(x) * g_ref[...]).astype(BF16)
    q = _dot(h, wq_ref[...])
    scale = HEAD_DIM ** -0.5
    heads = []
    for hd in range(N_HEADS):
        sl = slice(hd * HEAD_DIM, (hd + 1) * HEAD_DIM)
        qn = (_rms(q[:, sl]) * qg_ref[...] * scale).astype(BF16)
        s = _dot_nt(qn, k_ref[:, sl])
        p = jnp.exp(s - jnp.max(s, axis=-1, keepdims=True))
        o = _dot(p.astype(BF16), v_ref[:, sl])
        heads.append(o / jnp.sum(p, axis=-1, keepdims=True))
    a = jnp.concatenate(heads, axis=-1).astype(BF16)
    o_ref[...] = x + _dot(a, wo_ref[...])


def _xattn(x, g, wq, qg, k, v, wo):
    per_seq = SEQ // ROW_TILE
    return pl.pallas_call(
        _xattn_kernel,
        grid=(TOKENS // ROW_TILE,),
        in_specs=[pl.BlockSpec((ROW_TILE, D_MODEL), lambda m: (m, 0)),
                  pl.BlockSpec((1, D_MODEL), lambda m: (0, 0)),
                  pl.BlockSpec((D_MODEL, GROUP_WIDTH), lambda m: (0, 0)),
                  pl.BlockSpec((1, HEAD_DIM), lambda m: (0, 0)),
                  pl.BlockSpec((MEM_LEN, GROUP_WIDTH), lambda m: (m // per_seq, 0)),
                  pl.BlockSpec((MEM_LEN, GROUP_WIDTH), lambda m: (m // per_seq, 0)),
                  pl.BlockSpec((GROUP_WIDTH, D_MODEL), lambda m: (0, 0))],
        out_specs=pl.BlockSpec((ROW_TILE, D_MODEL), lambda m: (m, 0)),
        out_shape=jax.ShapeDtypeStruct((TOKENS, D_MODEL), F32),
        compiler_params=_cparams(("arbitrary",)),
        name="xattn",
    )(x, g, wq, qg, k, v, wo)


def _ffn_kernel(x_ref, g_ref, wg_ref, wu_ref, wd_ref, o_ref, h_ref):
    @pl.when(pl.program_id(1) == 0)
    def _():
        x = x_ref[...]
        h_ref[...] = (_rms(x) * g_ref[...]).astype(BF16)
        o_ref[...] = x

    h = h_ref[...]
    gate = _dot(h, wg_ref[...])
    up = _dot(h, wu_ref[...])
    a = (gate * (1.0 / (1.0 + jnp.exp(-gate))) * up).astype(BF16)
    o_ref[...] += _dot(a, wd_ref[...])


def _ffn(x, g, wgu, wd):
    bm, bf = FFN_BM, FFN_BF
    nf = D_FF // bf
    return pl.pallas_call(
        _ffn_kernel,
        grid=(TOKENS // bm, nf),
        in_specs=[pl.BlockSpec((bm, D_MODEL), lambda m, f: (m, 0)),
                  pl.BlockSpec((1, D_MODEL), lambda m, f: (0, 0)),
                  pl.BlockSpec((D_MODEL, bf), lambda m, f: (0, f)),
                  pl.BlockSpec((D_MODEL, bf), lambda m, f: (0, nf + f)),
                  pl.BlockSpec((bf, D_MODEL), lambda m, f: (f, 0))],
        out_specs=pl.BlockSpec((bm, D_MODEL), lambda m, f: (m, 0)),
        out_shape=jax.ShapeDtypeStruct((TOKENS, D_MODEL), F32),
        scratch_shapes=[pltpu.VMEM((bm, D_MODEL), BF16)],
        compiler_params=_cparams(("arbitrary", "arbitrary")),
        name="ffn",
    )(x, g, wgu, wgu, wd)


def _rot_cols(w):
    half = MLA_ROPE_DIM // 2
    return jnp.concatenate([-w[..., half:], w[..., :half]], axis=-1)


def _swap_halves(g):
    half = MLA_ROPE_DIM // 2
    return jnp.concatenate([g[half:], g[:half]])


def _rope_tables():
    pos = jnp.arange(SEQ, dtype=F32)
    inv_freq = ROPE_THETA ** (-jnp.arange(0, MLA_ROPE_DIM, 2, dtype=F32) / MLA_ROPE_DIM)
    ang = pos[:, None] * inv_freq[None, :]
    zeros = jnp.zeros((SEQ, MLA_ROPE_DIM), F32)
    cos = jnp.concatenate([jnp.cos(ang), jnp.cos(ang), zeros], axis=-1)
    sin = jnp.concatenate([jnp.sin(ang), jnp.sin(ang), zeros], axis=-1)
    return cos, sin


def _row(v):
    return v.reshape(1, -1).astype(F32)


def _pad_lanes(v, width):
    return jnp.pad(v.astype(F32), (0, width - v.shape[0])).reshape(1, width)


def _layer_params(l, p):
    w_in = p['w_in'][l]
    off = [0]
    for n in (512, 512, 512, 4, 512, 256, 64, 512, 512, 512, 512, 512):
        off.append(off[-1] + n)
    seg = lambda k: w_in[:, off[k]:off[k + 1]]
    fq, fk, fv, ff, mcq, mckv, mkr, gu, gv, bq, bk, bv = (seg(k) for k in range(12))
    mkr_rot = _rot_cols(mkr)
    w_main = jnp.concatenate([fq, fk, fv, mcq, gu, gv, bq, bk, bv, mckv,
                              mkr, mkr, mkr_rot, mkr_rot], axis=1).astype(BF16)
    w_small = jnp.pad(ff, ((0, 0), (0, SMALL_COLS - N_HEADS))).astype(BF16)

    wuq = p['mla_w_uq'][l].reshape(MLA_Q_RANK, N_HEADS, MLA_QK_DIM)
    nope, pe = wuq[..., :MLA_NOPE_DIM], wuq[..., MLA_NOPE_DIM:]
    rot = _rot_cols(pe)
    wuq = jnp.concatenate([nope, pe, pe, rot, rot], axis=-1).reshape(
        MLA_Q_RANK, N_HEADS * MLA_Q_HEAD_COLS).astype(BF16)
    wukv = p['mla_w_ukv'][l].reshape(MLA_KV_RANK, N_HEADS, 2 * HEAD_DIM)
    wukv = jnp.concatenate([wukv[..., :HEAD_DIM].reshape(MLA_KV_RANK, -1),
                            wukv[..., HEAD_DIM:].reshape(MLA_KV_RANK, -1)], axis=1).astype(BF16)
    qg, kg = p['mla_q_norm'][l], p['mla_k_norm'][l]
    two = lambda g: _row(jnp.concatenate([g, g]))

    bs = jnp.repeat(p['gmlp_b_s'][l].T, HEAD_DIM, axis=1).astype(F32)

    return dict(
        mix_norm=_row(p['mix_norm'][l]), w_main=w_main, w_small=w_small,
        fox_b_f=_pad_lanes(p['fox_b_f'][l], SMALL_COLS),
        fox_q=_row(p['fox_q_norm'][l]), fox_k=_row(p['fox_k_norm'][l]),
        mla_qlg=_row(p['mla_q_lora_norm'][l]), mla_wuq=wuq,
        mla_kvlg=_row(p['mla_kv_lora_norm'][l]), mla_wukv=wukv,
        mla_qgn=_row(qg[:MLA_NOPE_DIM]), mla_qga=two(qg[MLA_NOPE_DIM:]),
        mla_qgb=two(_swap_halves(qg[MLA_NOPE_DIM:])),
        mla_kgn=_row(kg[:MLA_NOPE_DIM]), mla_kga=two(kg[MLA_NOPE_DIM:]),
        mla_kgb=two(_swap_halves(kg[MLA_NOPE_DIM:])),
        gmlp_lg=_row(p['gmlp_ln_g'][l]), gmlp_lb=_row(p['gmlp_ln_b'][l]),
        gmlp_ws=p['gmlp_w_s'][l].astype(F32), gmlp_bs=bs,
        moba_q=_row(p['moba_q_norm'][l]), moba_k=_row(p['moba_k_norm'][l]),
        gn=[_row(p['group_norm'][l, g]) for g in range(4)],
        w_out=p['w_out'][l].astype(BF16),
        xattn_norm=_row(p['xattn_norm'][l]), mem_norm=_row(p['mem_norm'][l]),
        w_xq=p['w_xq'][l].astype(BF16), w_xkv=p['w_xkv'][l].astype(BF16),
        xq=_row(p['xattn_q_norm'][l]), xk=_row(p['xattn_k_norm'][l]),
        w_xo=p['w_xo'][l].astype(BF16),
        ffn_norm=_row(p['ffn_norm'][l]),
        w_gate_up=p['w_gate_up'][l].astype(BF16), w_down=p['w_down'][l].astype(BF16),
    )


def kernel(x, mem, mix_norm, w_in, fox_b_f, fox_q_norm, fox_k_norm, mla_q_lora_norm, mla_w_uq,
           mla_kv_lora_norm, mla_w_ukv, mla_q_norm, mla_k_norm, gmlp_ln_g, gmlp_ln_b, gmlp_w_s,
           gmlp_b_s, moba_q_norm, moba_k_norm, group_norm, w_out, xattn_norm, mem_norm, w_xq, w_xkv,
           xattn_q_norm, xattn_k_norm, w_xo, ffn_norm, w_gate_up, w_down):
    params = dict(mix_norm=mix_norm, w_in=w_in, fox_b_f=fox_b_f, fox_q_norm=fox_q_norm,
                  fox_k_norm=fox_k_norm, mla_q_lora_norm=mla_q_lora_norm, mla_w_uq=mla_w_uq,
                  mla_kv_lora_norm=mla_kv_lora_norm, mla_w_ukv=mla_w_ukv, mla_q_norm=mla_q_norm,
                  mla_k_norm=mla_k_norm, gmlp_ln_g=gmlp_ln_g, gmlp_ln_b=gmlp_ln_b,
                  gmlp_w_s=gmlp_w_s, gmlp_b_s=gmlp_b_s, moba_q_norm=moba_q_norm,
                  moba_k_norm=moba_k_norm, group_norm=group_norm, w_out=w_out,
                  xattn_norm=xattn_norm, mem_norm=mem_norm, w_xq=w_xq, w_xkv=w_xkv,
                  xattn_q_norm=xattn_q_norm, xattn_k_norm=xattn_k_norm, w_xo=w_xo,
                  ffn_norm=ffn_norm, w_gate_up=w_gate_up, w_down=w_down)
    cos, sin = _rope_tables()
    xf = x.reshape(TOKENS, D_MODEL)
    memf = mem.reshape(BATCH * MEM_LEN, D_MODEL)
    for l in range(DEPTH):
        q = _layer_params(l, params)
        proj, small = _inproj(xf, q['mix_norm'], q['w_main'], q['w_small'])
        o_a = _fox(proj, small, q['fox_b_f'], q['fox_q'], q['fox_k'], q['gn'][0])
        o_b = _mla(proj, cos, sin, q['mla_qlg'], q['mla_wuq'], q['mla_kvlg'], q['mla_wukv'],
                   q['mla_qgn'], q['mla_qga'], q['mla_qgb'], q['mla_kgn'], q['mla_kga'],
                   q['mla_kgb'], q['gn'][1])
        o_c = _gmlp(proj, q['gmlp_lg'], q['gmlp_lb'], q['gmlp_ws'], q['gmlp_bs'], q['gn'][2])
        o_d = _moba(proj, q['moba_q'], q['moba_k'], q['gn'][3])
        xf = _outproj(xf, o_a, o_b, o_c, o_d, q['w_out'])
        mk, mv = _memkv(memf, q['mem_norm'], q['w_xkv'], q['xk'])
        xf = _xattn(xf, q['xattn_norm'], q['w_xq'], q['xq'], mk, mv, q['w_xo'])
        xf = _ffn(xf, q['ffn_norm'], q['w_gate_up'], q['w_down'])
    return xf.reshape(BATCH, SEQ, D_MODEL)
```

```python
import functools

import jax
import jax.numpy as jnp
from jax import lax
from jax.experimental import pallas as pl
from jax.experimental.pallas import tpu as pltpu

F32 = jnp.float32
BF16 = jnp.bfloat16

D_MODEL = 2048
BATCH = 8
SEQ = 2048
DEPTH = 2
TOKENS = BATCH * SEQ
MEM_LEN = 256
GROUP_WIDTH = 512
HEAD_DIM = 128
N_HEADS = 4
MLA_Q_RANK = 512
MLA_KV_RANK = 256
MLA_NOPE_DIM = 128
MLA_ROPE_DIM = 64
MLA_QK_DIM = MLA_NOPE_DIM + MLA_ROPE_DIM
ROPE_THETA = 10000.0
GMLP_CHUNK = 128
MOBA_BLOCK = 256
MOBA_TOPK = 3
N_MOBA_BLOCKS = SEQ // MOBA_BLOCK
D_FF = 5632
EPS = 1e-6
NEG = -1e30

V7X_VMEM_LIMIT_BYTES = 56 * 1024 * 1024

COL_FQ, COL_FK, COL_FV = 0, 512, 1024
COL_MCQ = 1536
COL_GU, COL_GV = 2048, 2560
COL_BQ, COL_BK, COL_BV = 3072, 3584, 4096
COL_MCKV = 4608
COL_KROPE = 4864
PROJ_COLS = 5120
SMALL_COLS = 128

TQ = 256
TK = 256
NQ = SEQ // TQ
NKV = SEQ // TK
INPROJ_BM, INPROJ_BN = 1024, 1024
ROW_TILE = 512
FFN_BM, FFN_BF = 512, 512


def _cparams(sem):
    return pltpu.CompilerParams(dimension_semantics=sem,
                                vmem_limit_bytes=V7X_VMEM_LIMIT_BYTES)


def _rms(x):
    return x * lax.rsqrt(jnp.mean(x * x, axis=-1, keepdims=True) + EPS)


def _dot(a, b):
    return jnp.dot(a, b, preferred_element_type=F32)


def _dot_nt(a, b, precision=None):
    return lax.dot_general(a, b, (((1,), (1,)), ((), ())),
                           preferred_element_type=F32, precision=precision)


VT_ROWS = HEAD_DIM + 16


def _flash_scratch():
    return [pltpu.VMEM((2, N_HEADS, TK, TQ), F32), pltpu.VMEM((2, N_HEADS, 1, TQ), F32),
            pltpu.VMEM((N_HEADS, 1, TQ), F32), pltpu.VMEM((N_HEADS, VT_ROWS, TQ), F32)]


def _flash_heads(i, q, kv, scratch, keep=None):
    s_ref, cm_ref, m_ref, acc_ref = scratch
    key_idx = lax.broadcasted_iota(jnp.int32, (TK, TQ), 0)
    qry_idx = lax.broadcasted_iota(jnp.int32, (TK, TQ), 1)

    def stage_a(j, slot, diag=False):
        raw = [_dot_nt(kv(h, j)[0], q[h]) for h in range(N_HEADS)]
        for h in range(N_HEADS):
            s, bias = raw[h], kv(h, j)[2]
            if bias is not None:
                s = s - bias
            if diag:
                s = jnp.where(key_idx <= qry_idx, s, NEG)
            elif keep is not None:
                s = jnp.where(keep(h, j) > 0.5, s, NEG)
            s_ref[slot, h] = s
            cm_ref[slot, h] = jnp.max(s, axis=0, keepdims=True)

    def stage_b(j, slot, gate=None):
        probs, alphas = [], []
        for h in range(N_HEADS):
            m_old = m_ref[h]
            m_new = jnp.maximum(m_old, cm_ref[slot, h])
            m_ref[h] = m_new
            probs.append(jnp.exp(s_ref[slot, h] - m_new).astype(BF16))
            alphas.append(jnp.exp(m_old - m_new))
        for h in range(N_HEADS):
            pv = _dot(kv(h, j)[1], probs[h])
            if gate is not None:
                pv = pv * gate
            acc_ref[h] = alphas[h] * acc_ref[h] + pv

    m_ref[...] = jnp.full(m_ref.shape, NEG, F32)
    acc_ref[...] = jnp.zeros(acc_ref.shape, F32)
    last = jnp.maximum(i - 1, 0)
    stage_a(i, 0, diag=True)
    stage_a(0, 1)
    stage_b(i, 0)

    def body(p, _):
        t0 = 2 * p
        t1 = jnp.minimum(t0 + 1, last)
        stage_a(t1, 0)
        stage_b(t0, 1)
        stage_a(jnp.minimum(t0 + 2, last), 1)
        stage_b(t1, 0, gate=(t0 + 1 < i).astype(F32))
        return 0

    lax.fori_loop(0, (i + 1) // 2, body, 0)
    return [(acc_ref[h, :HEAD_DIM] / acc_ref[h, HEAD_DIM:HEAD_DIM + 1]).T for h in range(N_HEADS)]


def _store_vt(vt_ref, h, first_tile, v32):
    base = h * VT_ROWS
    for t in range(v32.shape[0] // TK):
        vt_ref[first_tile + t, base:base + HEAD_DIM, :] = v32[t * TK:(t + 1) * TK].T.astype(BF16)
        vt_ref[first_tile + t, base + HEAD_DIM:base + VT_ROWS, :] = jnp.ones((VT_ROWS - HEAD_DIM, TK), BF16)


def _group_out(heads, gn_ref, o_ref):
    o = jnp.concatenate(heads, axis=-1)
    o_ref[...] = (_rms(o) * gn_ref[...]).astype(o_ref.dtype)


def _inproj_kernel(x_ref, g_ref, w_ref, ws_ref, o_ref, os_ref, h_ref):
    @pl.when(pl.program_id(1) == 0)
    def _():
        h = (_rms(x_ref[...]) * g_ref[...]).astype(BF16)
        h_ref[...] = h
        os_ref[...] = _dot(h, ws_ref[...])

    o_ref[...] = _dot(h_ref[...], w_ref[...]).astype(o_ref.dtype)


def _inproj(x, g, w, ws):
    bm, bn = INPROJ_BM, INPROJ_BN
    return pl.pallas_call(
        _inproj_kernel,
        grid=(TOKENS // bm, PROJ_COLS // bn),
        in_specs=[pl.BlockSpec((bm, D_MODEL), lambda m, n: (m, 0)),
                  pl.BlockSpec((1, D_MODEL), lambda m, n: (0, 0)),
                  pl.BlockSpec((D_MODEL, bn), lambda m, n: (0, n)),
                  pl.BlockSpec((D_MODEL, SMALL_COLS), lambda m, n: (0, 0))],
        out_specs=[pl.BlockSpec((bm, bn), lambda m, n: (m, n)),
                   pl.BlockSpec((bm, SMALL_COLS), lambda m, n: (m, 0))],
        out_shape=[jax.ShapeDtypeStruct((TOKENS, PROJ_COLS), BF16),
                   jax.ShapeDtypeStruct((TOKENS, SMALL_COLS), F32)],
        scratch_shapes=[pltpu.VMEM((bm, D_MODEL), BF16)],
        compiler_params=_cparams(("arbitrary", "arbitrary")),
        name="inproj",
    )(x, g, w, ws)


def _fox_kernel(q_ref, k_ref, v_ref, f_ref, bf_ref, qg_ref, kg_ref, gn_ref, o_ref,
                kn_ref, vt_ref, c_ref, *flash):
    i = pl.program_id(1)
    scale = HEAD_DIM ** -0.5
    sls = [slice(h * HEAD_DIM, (h + 1) * HEAD_DIM) for h in range(N_HEADS)]

    @pl.when(i == 0)
    def _prep():
        x = f_ref[...] + bf_ref[...]
        c = jnp.minimum(x, 0.0) - jnp.log1p(jnp.exp(-jnp.abs(x)))
        row = lax.broadcasted_iota(jnp.int32, c.shape, 0)
        step = 1
        while step < SEQ:
            c = c + jnp.where(row >= step, pltpu.roll(c, step, 0), 0.0)
            step *= 2
        for h in range(N_HEADS):
            kn_ref[:, sls[h]] = (_rms(k_ref[:, sls[h]].astype(F32)) * kg_ref[...]).astype(BF16)
            _store_vt(vt_ref, h, 0, v_ref[:, sls[h]].astype(F32))
            c_ref[h] = jnp.broadcast_to(c[:, h:h + 1], (SEQ, HEAD_DIM))

    q = [(_rms(q_ref[:, sl].astype(F32)) * qg_ref[...] * scale).astype(BF16) for sl in sls]

    def kv(h, j):
        off = pl.multiple_of(j * TK, TK)
        c = c_ref[h, pl.ds(off, TK), :]
        return (kn_ref[pl.ds(off, TK), sls[h]],
                vt_ref[j, h * VT_ROWS:(h + 1) * VT_ROWS, :],
                jnp.concatenate([c] * (TQ // HEAD_DIM), axis=1))

    _group_out(_flash_heads(i, q, kv, flash), gn_ref, o_ref)


def _fox(proj, small, b_f, qg, kg, gn):
    row = lambda w: pl.BlockSpec((1, w), lambda b, i: (0, 0))
    return pl.pallas_call(
        _fox_kernel,
        grid=(BATCH, NQ),
        in_specs=[pl.BlockSpec((TQ, GROUP_WIDTH), lambda b, i: (b * NQ + i, COL_FQ // GROUP_WIDTH)),
                  pl.BlockSpec((SEQ, GROUP_WIDTH), lambda b, i: (b, COL_FK // GROUP_WIDTH)),
                  pl.BlockSpec((SEQ, GROUP_WIDTH), lambda b, i: (b, COL_FV // GROUP_WIDTH)),
                  pl.BlockSpec((SEQ, SMALL_COLS), lambda b, i: (b, 0)),
                  row(SMALL_COLS), row(HEAD_DIM), row(HEAD_DIM), row(GROUP_WIDTH)],
        out_specs=pl.BlockSpec((TQ, GROUP_WIDTH), lambda b, i: (b * NQ + i, 0)),
        out_shape=jax.ShapeDtypeStruct((TOKENS, GROUP_WIDTH), BF16),
        scratch_shapes=[pltpu.VMEM((SEQ, GROUP_WIDTH), BF16),
                        pltpu.VMEM((NKV, N_HEADS * VT_ROWS, TK), BF16),
                        pltpu.VMEM((N_HEADS, SEQ, HEAD_DIM), F32)] + _flash_scratch(),
        compiler_params=_cparams(("arbitrary", "arbitrary")),
        name="fox",
    )(proj, proj, proj, small, b_f, qg, kg, gn)


MLA_Q_HEAD_COLS = 384
MLA_K_COLS = 256


def _rope128(a, b, ga, gb, cos, sin):
    r = lax.rsqrt(jnp.sum(a * a, axis=-1, keepdims=True) * (1.0 / (2 * MLA_ROPE_DIM)) + EPS)
    return r * (a * ga * cos + b * gb * sin)


def _mla_kernel(cq_ref, ckv_ref, kr_ref, cos_ref, sin_ref, cosq_ref, sinq_ref,
                qlg_ref, wuq_ref, kvlg_ref, wukv_ref, qgn_ref, qga_ref, qgb_ref,
                kgn_ref, kga_ref, kgb_ref, gn_ref, o_ref, kf_ref, vt_ref, *flash):
    i = pl.program_id(1)
    scale = MLA_QK_DIM ** -0.5
    chunk = 512

    @pl.when(i == 0)
    def _prep():
        for c in range(SEQ // chunk):
            rows = slice(c * chunk, (c + 1) * chunk)
            ckv = (_rms(ckv_ref[rows, :].astype(F32)) * kvlg_ref[...]).astype(BF16)
            kv = _dot(ckv, wukv_ref[...])
            kr = kr_ref[rows, :].astype(F32)
            kpe = _rope128(kr[:, :128], kr[:, 128:], kga_ref[...], kgb_ref[...],
                           cos_ref[rows, :], sin_ref[rows, :]).astype(BF16)
            for h in range(N_HEADS):
                sl = slice(h * HEAD_DIM, (h + 1) * HEAD_DIM)
                kf_ref[h, rows, 0:128] = (_rms(kv[:, sl]) * kgn_ref[...]).astype(BF16)
                kf_ref[h, rows, 128:256] = kpe
                _store_vt(vt_ref, h, c * (chunk // TK),
                          kv[:, GROUP_WIDTH + h * HEAD_DIM:GROUP_WIDTH + (h + 1) * HEAD_DIM])

    cq = (_rms(cq_ref[...].astype(F32)) * qlg_ref[...]).astype(BF16)
    qraw = _dot(cq, wuq_ref[...])
    qf = []
    for h in range(N_HEADS):
        base = h * MLA_Q_HEAD_COLS
        qn = _rms(qraw[:, base:base + 128]) * qgn_ref[...]
        qpe = _rope128(qraw[:, base + 128:base + 256], qraw[:, base + 256:base + 384],
                       qga_ref[...], qgb_ref[...], cosq_ref[...], sinq_ref[...])
        qf.append((jnp.concatenate([qn, qpe], axis=-1) * scale).astype(BF16))

    def kv(h, j):
        off = pl.multiple_of(j * TK, TK)
        return kf_ref[h, pl.ds(off, TK), :], vt_ref[j, h * VT_ROWS:(h + 1) * VT_ROWS, :], None

    _group_out(_flash_heads(i, qf, kv, flash), gn_ref, o_ref)


def _mla(proj, cos, sin, qlg, wuq, kvlg, wukv, qgn, qga, qgb, kgn, kga, kgb, gn):
    full = lambda a: pl.BlockSpec(a.shape, lambda b, i: (0,) * a.ndim)
    return pl.pallas_call(
        _mla_kernel,
        grid=(BATCH, NQ),
        in_specs=[pl.BlockSpec((TQ, MLA_Q_RANK), lambda b, i: (b * NQ + i, COL_MCQ // MLA_Q_RANK)),
                  pl.BlockSpec((SEQ, MLA_KV_RANK), lambda b, i: (b, COL_MCKV // MLA_KV_RANK)),
                  pl.BlockSpec((SEQ, 256), lambda b, i: (b, COL_KROPE // 256)),
                  full(cos), full(sin),
                  pl.BlockSpec((TQ, 128), lambda b, i: (i, 0)),
                  pl.BlockSpec((TQ, 128), lambda b, i: (i, 0)),
                  full(qlg), full(wuq), full(kvlg), full(wukv), full(qgn), full(qga), full(qgb),
                  full(kgn), full(kga), full(kgb), full(gn)],
        out_specs=pl.BlockSpec((TQ, GROUP_WIDTH), lambda b, i: (b * NQ + i, 0)),
        out_shape=jax.ShapeDtypeStruct((TOKENS, GROUP_WIDTH), BF16),
        scratch_shapes=[pltpu.VMEM((N_HEADS, SEQ, MLA_K_COLS), BF16),
                        pltpu.VMEM((NKV, N_HEADS * VT_ROWS, TK), BF16)] + _flash_scratch(),
        compiler_params=_cparams(("arbitrary", "arbitrary")),
        name="mla",
    )(proj, proj, proj, cos, sin, cos, sin, qlg, wuq, kvlg, wukv, qgn, qga, qgb, kgn, kga, kgb, gn)


def _gelu(x):
    return 0.5 * x * (1.0 + jnp.tanh(0.7978845608028654 * (x + 0.044715 * (x * x * x))))


def _gmlp_kernel(u_ref, v_ref, lg_ref, lb_ref, ws_ref, bs_ref, gn_ref, o_ref, mix_ref):
    v = _gelu(v_ref[...].astype(F32))
    mu = jnp.mean(v, axis=-1, keepdims=True)
    vc = v - mu
    var = jnp.mean(vc * vc, axis=-1, keepdims=True)
    vn = (vc * lax.rsqrt(var + EPS) * lg_ref[...] + lb_ref[...]).astype(BF16)
    r = lax.broadcasted_iota(jnp.int32, (GMLP_CHUNK, GMLP_CHUNK), 0)
    c = lax.broadcasted_iota(jnp.int32, (GMLP_CHUNK, GMLP_CHUNK), 1)
    for g in range(GROUP_WIDTH // HEAD_DIM):
        w = jnp.where(c <= r, ws_ref[g], 0.0).astype(BF16)
        cols = slice(g * HEAD_DIM, (g + 1) * HEAD_DIM)
        for n in range(ROW_TILE // GMLP_CHUNK):
            rows = slice(n * GMLP_CHUNK, (n + 1) * GMLP_CHUNK)
            mix_ref[rows, cols] = _dot(w, vn[rows, cols]) + bs_ref[:, cols]
    o = _gelu(u_ref[...].astype(F32)) * mix_ref[...]
    o_ref[...] = (_rms(o) * gn_ref[...]).astype(o_ref.dtype)


def _gmlp(proj, lg, lb, ws, bs, gn):
    full = lambda a: pl.BlockSpec(a.shape, lambda m: (0,) * a.ndim)
    return pl.pallas_call(
        _gmlp_kernel,
        grid=(TOKENS // ROW_TILE,),
        in_specs=[pl.BlockSpec((ROW_TILE, GROUP_WIDTH), lambda m: (m, COL_GU // GROUP_WIDTH)),
                  pl.BlockSpec((ROW_TILE, GROUP_WIDTH), lambda m: (m, COL_GV // GROUP_WIDTH)),
                  full(lg), full(lb), full(ws), full(bs), full(gn)],
        out_specs=pl.BlockSpec((ROW_TILE, GROUP_WIDTH), lambda m: (m, 0)),
        out_shape=jax.ShapeDtypeStruct((TOKENS, GROUP_WIDTH), BF16),
        scratch_shapes=[pltpu.VMEM((ROW_TILE, GROUP_WIDTH), F32)],
        compiler_params=_cparams(("arbitrary",)),
        name="gmlp",
    )(proj, proj, lg, lb, ws, bs, gn)


def _moba_kernel(q_ref, k_ref, v_ref, qg_ref, kg_ref, gn_ref, o_ref, kn_ref, vt_ref, km_ref,
                 sel_ref, *flash):
    i = pl.program_id(1)
    scale = HEAD_DIM ** -0.5

    @pl.when(i == 0)
    def _prep():
        for h in range(N_HEADS):
            sl = slice(h * HEAD_DIM, (h + 1) * HEAD_DIM)
            means = []
            for n in range(N_MOBA_BLOCKS):
                rows = slice(n * MOBA_BLOCK, (n + 1) * MOBA_BLOCK)
                kn = _rms(k_ref[rows, sl].astype(F32)) * kg_ref[...]
                kn_ref[rows, sl] = kn.astype(BF16)
                means.append(jnp.mean(kn, axis=0, keepdims=True))
            km_ref[h] = jnp.concatenate(means, axis=0)
            _store_vt(vt_ref, h, 0, v_ref[:, sl].astype(F32))

    blk = lax.broadcasted_iota(jnp.int32, (N_MOBA_BLOCKS, TQ), 0)
    sls = [slice(h * HEAD_DIM, (h + 1) * HEAD_DIM) for h in range(N_HEADS)]
    qn = []
    for h in range(N_HEADS):
        q32 = _rms(q_ref[:, sls[h]].astype(F32)) * qg_ref[...]
        qn.append((q32 * scale).astype(BF16))

        gate = _dot_nt(km_ref[h], q32, precision=lax.Precision.HIGHEST)
        gate = jnp.where(blk < i, gate, NEG)
        sel_t = jnp.zeros((N_MOBA_BLOCKS, TQ), F32)
        for n in range(N_MOBA_BLOCKS):
            gn_row = gate[n:n + 1, :]
            beats = jnp.where(gate > gn_row, 1.0, jnp.where((gate == gn_row) & (blk < n), 1.0, 0.0))
            rank = jnp.sum(beats, axis=0, keepdims=True)
            past = jnp.full((1, TQ), n, jnp.int32) < i
            chosen = jnp.where(rank < MOBA_TOPK, jnp.where(past, 1.0, 0.0), 0.0)
            sel_t = jnp.where(blk == n, chosen, sel_t)
        sel_ref[h] = sel_t

    def kv(h, j):
        off = pl.multiple_of(j * TK, TK)
        return kn_ref[pl.ds(off, TK), sls[h]], vt_ref[j, h * VT_ROWS:(h + 1) * VT_ROWS, :], None

    def keep(h, j):
        return sel_ref[h, pl.ds(j, 1), :]

    _group_out(_flash_heads(i, qn, kv, flash, keep), gn_ref, o_ref)


def _moba(proj, qg, kg, gn):
    row = lambda w: pl.BlockSpec((1, w), lambda b, i: (0, 0))
    return pl.pallas_call(
        _moba_kernel,
        grid=(BATCH, NQ),
        in_specs=[pl.BlockSpec((TQ, GROUP_WIDTH), lambda b, i: (b * NQ + i, COL_BQ // GROUP_WIDTH)),
                  pl.BlockSpec((SEQ, GROUP_WIDTH), lambda b, i: (b, COL_BK // GROUP_WIDTH)),
                  pl.BlockSpec((SEQ, GROUP_WIDTH), lambda b, i: (b, COL_BV // GROUP_WIDTH)),
                  row(HEAD_DIM), row(HEAD_DIM), row(GROUP_WIDTH)],
        out_specs=pl.BlockSpec((TQ, GROUP_WIDTH), lambda b, i: (b * NQ + i, 0)),
        out_shape=jax.ShapeDtypeStruct((TOKENS, GROUP_WIDTH), BF16),
        scratch_shapes=[pltpu.VMEM((SEQ, GROUP_WIDTH), BF16),
                        pltpu.VMEM((NKV, N_HEADS * VT_ROWS, TK), BF16),
                        pltpu.VMEM((N_HEADS, N_MOBA_BLOCKS, HEAD_DIM), F32),
                        pltpu.VMEM((N_HEADS, N_MOBA_BLOCKS, TQ), F32)] + _flash_scratch(),
        compiler_params=_cparams(("arbitrary", "arbitrary")),
        name="moba",
    )(proj, proj, proj, qg, kg, gn)


def _outproj_kernel(x_ref, a_ref, b_ref, c_ref, d_ref, w_ref, o_ref):
    mixed = jnp.concatenate([a_ref[...], b_ref[...], c_ref[...], d_ref[...]], axis=-1)
    o_ref[...] = x_ref[...] + _dot(mixed, w_ref[...])


def _outproj(x, oa, ob, oc, od, w):
    grp = pl.BlockSpec((ROW_TILE, GROUP_WIDTH), lambda m: (m, 0))
    return pl.pallas_call(
        _outproj_kernel,
        grid=(TOKENS // ROW_TILE,),
        in_specs=[pl.BlockSpec((ROW_TILE, D_MODEL), lambda m: (m, 0)), grp, grp, grp, grp,
                  pl.BlockSpec((D_MODEL, D_MODEL), lambda m: (0, 0))],
        out_specs=pl.BlockSpec((ROW_TILE, D_MODEL), lambda m: (m, 0)),
        out_shape=jax.ShapeDtypeStruct((TOKENS, D_MODEL), F32),
        compiler_params=_cparams(("arbitrary",)),
        name="outproj",
    )(x, oa, ob, oc, od, w)


def _memkv_kernel(m_ref, g_ref, w_ref, kg_ref, k_ref, v_ref):
    h = (_rms(m_ref[...]) * g_ref[...]).astype(BF16)
    kv = _dot(h, w_ref[...])
    for hd in range(N_HEADS):
        sl = slice(hd * HEAD_DIM, (hd + 1) * HEAD_DIM)
        k_ref[:, sl] = (_rms(kv[:, sl]) * kg_ref[...]).astype(BF16)
    v_ref[...] = kv[:, GROUP_WIDTH:].astype(BF16)


def _memkv(mem, g, w, kg):
    out = pl.BlockSpec((MEM_LEN, GROUP_WIDTH), lambda b: (b, 0))
    return pl.pallas_call(
        _memkv_kernel,
        grid=(BATCH,),
        in_specs=[pl.BlockSpec((MEM_LEN, D_MODEL), lambda b: (b, 0)),
                  pl.BlockSpec((1, D_MODEL), lambda b: (0, 0)),
                  pl.BlockSpec((D_MODEL, 2 * GROUP_WIDTH), lambda b: (0, 0)),
                  pl.BlockSpec((1, HEAD_DIM), lambda b: (0, 0))],
        out_specs=[out, out],
        out_shape=[jax.ShapeDtypeStruct((BATCH * MEM_LEN, GROUP_WIDTH), BF16)] * 2,
        compiler_params=_cparams(("arbitrary",)),
        name="memkv",
    )(mem, g, w, kg)


def _xattn_kernel(x_ref, g_ref, wq_ref, qg_ref, k_ref, v_ref, wo_ref, o_ref):
    x = x_ref[...]
    h = (_rms(x) * g_ref[...]).astype(BF16)
    q = _dot(h, wq_ref[...])
    scale = HEAD_DIM ** -0.5
    heads = []
    for hd in range(N_HEADS):
        sl = slice(hd * HEAD_DIM, (hd + 1) * HEAD_DIM)
        qn = (_rms(q[:, sl]) * qg_ref[...] * scale).astype(BF16)
        s = _dot_nt(qn, k_ref[:, sl])
        p = jnp.exp(s - jnp.max(s, axis=-1, keepdims=True))
        o = _dot(p.astype(BF16), v_ref[:, sl])
        heads.append(o / jnp.sum(p, axis=-1, keepdims=True))
    a = jnp.concatenate(heads, axis=-1).astype(BF16)
    o_ref[...] = x + _dot(a, wo_ref[...])


def _xattn(x, g, wq, qg, k, v, wo):
    per_seq = SEQ // ROW_TILE
    return pl.pallas_call(
        _xattn_kernel,
        grid=(TOKENS // ROW_TILE,),
        in_specs=[pl.BlockSpec((ROW_TILE, D_MODEL), lambda m: (m, 0)),
                  pl.BlockSpec((1, D_MODEL), lambda m: (0, 0)),
                  pl.BlockSpec((D_MODEL, GROUP_WIDTH), lambda m: (0, 0)),
                  pl.BlockSpec((1, HEAD_DIM), lambda m: (0, 0)),
                  pl.BlockSpec((MEM_LEN, GROUP_WIDTH), lambda m: (m // per_seq, 0)),
                  pl.BlockSpec((MEM_LEN, GROUP_WIDTH), lambda m: (m // per_seq, 0)),
                  pl.BlockSpec((GROUP_WIDTH, D_MODEL), lambda m: (0, 0))],
        out_specs=pl.BlockSpec((ROW_TILE, D_MODEL), lambda m: (m, 0)),
        out_shape=jax.ShapeDtypeStruct((TOKENS, D_MODEL), F32),
        compiler_params=_cparams(("arbitrary",)),
        name="xattn",
    )(x, g, wq, qg, k, v, wo)


def _ffn_kernel(x_ref, g_ref, wg_ref, wu_ref, wd_ref, o_ref, h_ref):
    @pl.when(pl.program_id(1) == 0)
    def _():
        x = x_ref[...]
        h_ref[...] = (_rms(x) * g_ref[...]).astype(BF16)
        o_ref[...] = x

    h = h_ref[...]
    gate = _dot(h, wg_ref[...])
    up = _dot(h, wu_ref[...])
    a = (gate * (1.0 / (1.0 + jnp.exp(-gate))) * up).astype(BF16)
    o_ref[...] += _dot(a, wd_ref[...])


def _ffn(x, g, wgu, wd):
    bm, bf = FFN_BM, FFN_BF
    nf = D_FF // bf
    return pl.pallas_call(
        _ffn_kernel,
        grid=(TOKENS // bm, nf),
        in_specs=[pl.BlockSpec((bm, D_MODEL), lambda m, f: (m, 0)),
                  pl.BlockSpec((1, D_MODEL), lambda m, f: (0, 0)),
                  pl.BlockSpec((D_MODEL, bf), lambda m, f: (0, f)),
                  pl.BlockSpec((D_MODEL, bf), lambda m, f: (0, nf + f)),
                  pl.BlockSpec((bf, D_MODEL), lambda m, f: (f, 0))],
        out_specs=pl.BlockSpec((bm, D_MODEL), lambda m, f: (m, 0)),
        out_shape=jax.ShapeDtypeStruct((TOKENS, D_MODEL), F32),
        scratch_shapes=[pltpu.VMEM((bm, D_MODEL), BF16)],
        compiler_params=_cparams(("arbitrary", "arbitrary")),
        name="ffn",
    )(x, g, wgu, wgu, wd)


def _rot_cols(w):
    half = MLA_ROPE_DIM // 2
    return jnp.concatenate([-w[..., half:], w[..., :half]], axis=-1)


def _swap_halves(g):
    half = MLA_ROPE_DIM // 2
    return jnp.concatenate([g[half:], g[:half]])


def _rope_tables():
    pos = jnp.arange(SEQ, dtype=F32)
    inv_freq = ROPE_THETA ** (-jnp.arange(0, MLA_ROPE_DIM, 2, dtype=F32) / MLA_ROPE_DIM)
    ang = pos[:, None] * inv_freq[None, :]
    zeros = jnp.zeros((SEQ, MLA_ROPE_DIM), F32)
    cos = jnp.concatenate([jnp.cos(ang), jnp.cos(ang), zeros], axis=-1)
    sin = jnp.concatenate([jnp.sin(ang), jnp.sin(ang), zeros], axis=-1)
    return cos, sin


def _row(v):
    return v.reshape(1, -1).astype(F32)


def _pad_lanes(v, width):
    return jnp.pad(v.astype(F32), (0, width - v.shape[0])).reshape(1, width)


def _layer_params(l, p):
    w_in = p['w_in'][l]
    off = [0]
    for n in (512, 512, 512, 4, 512, 256, 64, 512, 512, 512, 512, 512):
        off.append(off[-1] + n)
    seg = lambda k: w_in[:, off[k]:off[k + 1]]
    fq, fk, fv, ff, mcq, mckv, mkr, gu, gv, bq, bk, bv = (seg(k) for k in range(12))
    mkr_rot = _rot_cols(mkr)
    w_main = jnp.concatenate([fq, fk, fv, mcq, gu, gv, bq, bk, bv, mckv,
                              mkr, mkr, mkr_rot, mkr_rot], axis=1).astype(BF16)
    w_small = jnp.pad(ff, ((0, 0), (0, SMALL_COLS - N_HEADS))).astype(BF16)

    wuq = p['mla_w_uq'][l].reshape(MLA_Q_RANK, N_HEADS, MLA_QK_DIM)
    nope, pe = wuq[..., :MLA_NOPE_DIM], wuq[..., MLA_NOPE_DIM:]
    rot = _rot_cols(pe)
    wuq = jnp.concatenate([nope, pe, pe, rot, rot], axis=-1).reshape(
        MLA_Q_RANK, N_HEADS * MLA_Q_HEAD_COLS).astype(BF16)
    wukv = p['mla_w_ukv'][l].reshape(MLA_KV_RANK, N_HEADS, 2 * HEAD_DIM)
    wukv = jnp.concatenate([wukv[..., :HEAD_DIM].reshape(MLA_KV_RANK, -1),
                            wukv[..., HEAD_DIM:].reshape(MLA_KV_RANK, -1)], axis=1).astype(BF16)
    qg, kg = p['mla_q_norm'][l], p['mla_k_norm'][l]
    two = lambda g: _row(jnp.concatenate([g, g]))

    bs = jnp.repeat(p['gmlp_b_s'][l].T, HEAD_DIM, axis=1).astype(F32)

    return dict(
        mix_norm=_row(p['mix_norm'][l]), w_main=w_main, w_small=w_small,
        fox_b_f=_pad_lanes(p['fox_b_f'][l], SMALL_COLS),
        fox_q=_row(p['fox_q_norm'][l]), fox_k=_row(p['fox_k_norm'][l]),
        mla_qlg=_row(p['mla_q_lora_norm'][l]), mla_wuq=wuq,
        mla_kvlg=_row(p['mla_kv_lora_norm'][l]), mla_wukv=wukv,
        mla_qgn=_row(qg[:MLA_NOPE_DIM]), mla_qga=two(qg[MLA_NOPE_DIM:]),
        mla_qgb=two(_swap_halves(qg[MLA_NOPE_DIM:])),
        mla_kgn=_row(kg[:MLA_NOPE_DIM]), mla_kga=two(kg[MLA_NOPE_DIM:]),
        mla_kgb=two(_swap_halves(kg[MLA_NOPE_DIM:])),
        gmlp_lg=_row(p['gmlp_ln_g'][l]), gmlp_lb=_row(p['gmlp_ln_b'][l]),
        gmlp_ws=p['gmlp_w_s'][l].astype(F32), gmlp_bs=bs,
        moba_q=_row(p['moba_q_norm'][l]), moba_k=_row(p['moba_k_norm'][l]),
        gn=[_row(p['group_norm'][l, g]) for g in range(4)],
        w_out=p['w_out'][l].astype(BF16),
        xattn_norm=_row(p['xattn_norm'][l]), mem_norm=_row(p['mem_norm'][l]),
        w_xq=p['w_xq'][l].astype(BF16), w_xkv=p['w_xkv'][l].astype(BF16),
        xq=_row(p['xattn_q_norm'][l]), xk=_row(p['xattn_k_norm'][l]),
        w_xo=p['w_xo'][l].astype(BF16),
        ffn_norm=_row(p['ffn_norm'][l]),
        w_gate_up=p['w_gate_up'][l].astype(BF16), w_down=p['w_down'][l].astype(BF16),
    )


def kernel(x, mem, mix_norm, w_in, fox_b_f, fox_q_norm, fox_k_norm, mla_q_lora_norm, mla_w_uq,
           mla_kv_lora_norm, mla_w_ukv, mla_q_norm, mla_k_norm, gmlp_ln_g, gmlp_ln_b, gmlp_w_s,
           gmlp_b_s, moba_q_norm, moba_k_norm, group_norm, w_out, xattn_norm, mem_norm, w_xq, w_xkv,
           xattn_q_norm, xattn_k_norm, w_xo, ffn_norm, w_gate_up, w_down):
    params = dict(mix_norm=mix_norm, w_in=w_in, fox_b_f=fox_b_f, fox_q_norm=fox_q_norm,
                  fox_k_norm=fox_k_norm, mla_q_lora_norm=mla_q_lora_norm, mla_w_uq=mla_w_uq,
                  mla_kv_lora_norm=mla_kv_lora_norm, mla_w_ukv=mla_w_ukv, mla_q_norm=mla_q_norm,
                  mla_k_norm=mla_k_norm, gmlp_ln_g=gmlp_ln_g, gmlp_ln_b=gmlp_ln_b,
                  gmlp_w_s=gmlp_w_s, gmlp_b_s=gmlp_b_s, moba_q_norm=moba_q_norm,
                  moba_k_norm=moba_k_norm, group_norm=group_norm, w_out=w_out,
                  xattn_norm=xattn_norm, mem_norm=mem_norm, w_xq=w_xq, w_xkv=w_xkv,
                  xattn_q_norm=xattn_q_norm, xattn_k_norm=xattn_k_norm, w_xo=w_xo,
                  ffn_norm=ffn_norm, w_gate_up=w_gate_up, w_down=w_down)
    cos, sin = _rope_tables()
    xf = x.reshape(TOKENS, D_MODEL)
    memf = mem.reshape(BATCH * MEM_LEN, D_MODEL)
    for l in range(DEPTH):
        q = _layer_params(l, params)
        proj, small = _inproj(xf, q['mix_norm'], q['w_main'], q['w_small'])
        o_a = _fox(proj, small, q['fox_b_f'], q['fox_q'], q['fox_k'], q['gn'][0])
        o_b = _mla(proj, cos, sin, q['mla_qlg'], q['mla_wuq'], q['mla_kvlg'], q['mla_wukv'],
                   q['mla_qgn'], q['mla_qga'], q['mla_qgb'], q['mla_kgn'], q['mla_kga'],
                   q['mla_kgb'], q['gn'][1])
        o_c = _gmlp(proj, q['gmlp_lg'], q['gmlp_lb'], q['gmlp_ws'], q['gmlp_bs'], q['gn'][2])
        o_d = _moba(proj, q['moba_q'], q['moba_k'], q['gn'][3])
        xf = _outproj(xf, o_a, o_b, o_c, o_d, q['w_out'])
        mk, mv = _memkv(memf, q['mem_norm'], q['w_xkv'], q['xk'])
        xf = _xattn(xf, q['xattn_norm'], q['w_xq'], q['xq'], mk, mv, q['w_xo'])
        xf = _ffn(xf, q['ffn_norm'], q['w_gate_up'], q['w_down'])
    return xf.reshape(BATCH, SEQ, D_MODEL)
```

```python
import functools

import jax
import jax.numpy as jnp
from jax import lax
from jax.experimental import pallas as pl
from jax.experimental.pallas import tpu as pltpu

F32 = jnp.float32
BF16 = jnp.bfloat16

D_MODEL = 2048
BATCH = 8
SEQ = 2048
DEPTH = 2
TOKENS = BATCH * SEQ
MEM_LEN = 256
GROUP_WIDTH = 512
HEAD_DIM = 128
N_HEADS = 4
MLA_Q_RANK = 512
MLA_KV_RANK = 256
MLA_NOPE_DIM = 128
MLA_ROPE_DIM = 64
MLA_QK_DIM = MLA_NOPE_DIM + MLA_ROPE_DIM
ROPE_THETA = 10000.0
GMLP_CHUNK = 128
MOBA_BLOCK = 256
MOBA_TOPK = 3
N_MOBA_BLOCKS = SEQ // MOBA_BLOCK
D_FF = 5632
EPS = 1e-6
NEG = -1e30
LOG2E = 1.4426950408889634

V7X_VMEM_LIMIT_BYTES = 56 * 1024 * 1024

COL_FQ, COL_FK, COL_FV = 0, 512, 1024
COL_MCQ = 1536
COL_GU, COL_GV = 2048, 2560
COL_BQ, COL_BK, COL_BV = 3072, 3584, 4096
COL_MCKV = 4608
COL_KROPE = 4864
PROJ_COLS = 5120
SMALL_COLS = 128

TQ = 256
TK = 256
NQ = SEQ // TQ
NKV = SEQ // TK
INPROJ_BM, INPROJ_BN = 1024, 1024
ROW_TILE = 512
FFN_BM, FFN_BF = 1024, 256


def _cparams(sem):
    return pltpu.CompilerParams(dimension_semantics=sem,
                                vmem_limit_bytes=V7X_VMEM_LIMIT_BYTES)


def _rms(x):
    return x * lax.rsqrt(jnp.mean(x * x, axis=-1, keepdims=True) + EPS)


def _dot(a, b):
    return jnp.dot(a, b, preferred_element_type=F32)


def _dot_nt(a, b, precision=None):
    return lax.dot_general(a, b, (((1,), (1,)), ((), ())),
                           preferred_element_type=F32, precision=precision)


VT_ROWS = HEAD_DIM + 16


def _flash_scratch():
    return [pltpu.VMEM((2, N_HEADS, TK, TQ), F32), pltpu.VMEM((2, N_HEADS, 1, TQ), F32),
            pltpu.VMEM((N_HEADS, 1, TQ), F32), pltpu.VMEM((N_HEADS, VT_ROWS, TQ), F32)]


def _flash_heads(i, q, kv, scratch, keep=None):
    s_ref, cm_ref, m_ref, acc_ref = scratch
    key_idx = lax.broadcasted_iota(jnp.int32, (TK, TQ), 0)
    qry_idx = lax.broadcasted_iota(jnp.int32, (TK, TQ), 1)

    def stage_a(j, slot, diag=False):
        raw = [_dot_nt(kv(h, j)[0], q[h]) for h in range(N_HEADS)]
        for h in range(N_HEADS):
            s, bias = raw[h], kv(h, j)[2]
            if bias is not None:
                s = s - bias
            if diag:
                s = jnp.where(key_idx <= qry_idx, s, NEG)
            elif keep is not None:
                s = jnp.where(keep(h, j) > 0.5, s, NEG)
            s_ref[slot, h] = s
            cm_ref[slot, h] = jnp.max(s, axis=0, keepdims=True)

    def stage_b(j, slot, gate=None):
        probs, alphas = [], []
        for h in range(N_HEADS):
            m_old = m_ref[h]
            m_new = jnp.maximum(m_old, cm_ref[slot, h])
            m_ref[h] = m_new
            probs.append(jnp.exp2(s_ref[slot, h] - m_new).astype(BF16))
            alphas.append(jnp.exp2(m_old - m_new))
        for h in range(N_HEADS):
            pv = _dot(kv(h, j)[1], probs[h])
            if gate is not None:
                pv = pv * gate
            acc_ref[h] = alphas[h] * acc_ref[h] + pv

    m_ref[...] = jnp.full(m_ref.shape, NEG, F32)
    acc_ref[...] = jnp.zeros(acc_ref.shape, F32)
    last = jnp.maximum(i - 1, 0)
    stage_a(i, 0, diag=True)
    stage_a(0, 1)
    stage_b(i, 0)

    def body(p, _):
        t0 = 2 * p
        t1 = jnp.minimum(t0 + 1, last)
        stage_a(t1, 0)
        stage_b(t0, 1)
        stage_a(jnp.minimum(t0 + 2, last), 1)
        stage_b(t1, 0, gate=(t0 + 1 < i).astype(F32))
        return 0

    lax.fori_loop(0, (i + 1) // 2, body, 0)
    return [(acc_ref[h, :HEAD_DIM] / acc_ref[h, HEAD_DIM:HEAD_DIM + 1]).T for h in range(N_HEADS)]


def _store_vt(vt_ref, h, first_tile, v32):
    base = h * VT_ROWS
    for t in range(v32.shape[0] // TK):
        vt_ref[first_tile + t, base:base + HEAD_DIM, :] = v32[t * TK:(t + 1) * TK].T.astype(BF16)
        vt_ref[first_tile + t, base + HEAD_DIM:base + VT_ROWS, :] = jnp.ones((VT_ROWS - HEAD_DIM, TK), BF16)


def _group_out(heads, gn_ref, o_ref):
    o = jnp.concatenate(heads, axis=-1)
    o_ref[...] = (_rms(o) * gn_ref[...]).astype(o_ref.dtype)


def _inproj_kernel(x_ref, g_ref, w_ref, ws_ref, o_ref, os_ref, h_ref):
    @pl.when(pl.program_id(1) == 0)
    def _():
        h = (_rms(x_ref[...]) * g_ref[...]).astype(BF16)
        h_ref[...] = h
        os_ref[...] = _dot(h, ws_ref[...])

    o_ref[...] = _dot(h_ref[...], w_ref[...]).astype(o_ref.dtype)


def _inproj(x, g, w, ws):
    bm, bn = INPROJ_BM, INPROJ_BN
    return pl.pallas_call(
        _inproj_kernel,
        grid=(TOKENS // bm, PROJ_COLS // bn),
        in_specs=[pl.BlockSpec((bm, D_MODEL), lambda m, n: (m, 0)),
                  pl.BlockSpec((1, D_MODEL), lambda m, n: (0, 0)),
                  pl.BlockSpec((D_MODEL, bn), lambda m, n: (0, n)),
                  pl.BlockSpec((D_MODEL, SMALL_COLS), lambda m, n: (0, 0))],
        out_specs=[pl.BlockSpec((bm, bn), lambda m, n: (m, n)),
                   pl.BlockSpec((bm, SMALL_COLS), lambda m, n: (m, 0))],
        out_shape=[jax.ShapeDtypeStruct((TOKENS, PROJ_COLS), BF16),
                   jax.ShapeDtypeStruct((TOKENS, SMALL_COLS), F32)],
        scratch_shapes=[pltpu.VMEM((bm, D_MODEL), BF16)],
        compiler_params=_cparams(("arbitrary", "arbitrary")),
        name="inproj",
    )(x, g, w, ws)


def _fox_kernel(q_ref, k_ref, v_ref, f_ref, bf_ref, qg_ref, kg_ref, gn_ref, o_ref,
                kn_ref, vt_ref, c_ref, *flash):
    i = pl.program_id(1)
    scale = HEAD_DIM ** -0.5 * LOG2E
    sls = [slice(h * HEAD_DIM, (h + 1) * HEAD_DIM) for h in range(N_HEADS)]

    @pl.when(i == 0)
    def _prep():
        x = f_ref[...] + bf_ref[...]
        c = jnp.minimum(x, 0.0) - jnp.log1p(jnp.exp(-jnp.abs(x)))
        row = lax.broadcasted_iota(jnp.int32, c.shape, 0)
        step = 1
        while step < SEQ:
            c = c + jnp.where(row >= step, pltpu.roll(c, step, 0), 0.0)
            step *= 2
        for h in range(N_HEADS):
            kn_ref[:, sls[h]] = (_rms(k_ref[:, sls[h]].astype(F32)) * kg_ref[...]).astype(BF16)
            _store_vt(vt_ref, h, 0, v_ref[:, sls[h]].astype(F32))
            c_ref[h] = jnp.broadcast_to(c[:, h:h + 1] * LOG2E, (SEQ, HEAD_DIM))

    q = [(_rms(q_ref[:, sl].astype(F32)) * qg_ref[...] * scale).astype(BF16) for sl in sls]

    def kv(h, j):
        off = pl.multiple_of(j * TK, TK)
        c = c_ref[h, pl.ds(off, TK), :]
        return (kn_ref[pl.ds(off, TK), sls[h]],
                vt_ref[j, h * VT_ROWS:(h + 1) * VT_ROWS, :],
                jnp.concatenate([c] * (TQ // HEAD_DIM), axis=1))

    _group_out(_flash_heads(i, q, kv, flash), gn_ref, o_ref)


def _fox(proj, small, b_f, qg, kg, gn):
    row = lambda w: pl.BlockSpec((1, w), lambda b, i: (0, 0))
    return pl.pallas_call(
        _fox_kernel,
        grid=(BATCH, NQ),
        in_specs=[pl.BlockSpec((TQ, GROUP_WIDTH), lambda b, i: (b * NQ + i, COL_FQ // GROUP_WIDTH)),
                  pl.BlockSpec((SEQ, GROUP_WIDTH), lambda b, i: (b, COL_FK // GROUP_WIDTH)),
                  pl.BlockSpec((SEQ, GROUP_WIDTH), lambda b, i: (b, COL_FV // GROUP_WIDTH)),
                  pl.BlockSpec((SEQ, SMALL_COLS), lambda b, i: (b, 0)),
                  row(SMALL_COLS), row(HEAD_DIM), row(HEAD_DIM), row(GROUP_WIDTH)],
        out_specs=pl.BlockSpec((TQ, GROUP_WIDTH), lambda b, i: (b * NQ + i, 0)),
        out_shape=jax.ShapeDtypeStruct((TOKENS, GROUP_WIDTH), BF16),
        scratch_shapes=[pltpu.VMEM((SEQ, GROUP_WIDTH), BF16),
                        pltpu.VMEM((NKV, N_HEADS * VT_ROWS, TK), BF16),
                        pltpu.VMEM((N_HEADS, SEQ, HEAD_DIM), F32)] + _flash_scratch(),
        compiler_params=_cparams(("arbitrary", "arbitrary")),
        name="fox",
    )(proj, proj, proj, small, b_f, qg, kg, gn)


MLA_Q_HEAD_COLS = 384
MLA_K_COLS = 256


def _rope128(a, b, ga, gb, cos, sin):
    r = lax.rsqrt(jnp.sum(a * a, axis=-1, keepdims=True) * (1.0 / (2 * MLA_ROPE_DIM)) + EPS)
    return r * (a * ga * cos + b * gb * sin)


def _mla_kernel(cq_ref, ckv_ref, kr_ref, cos_ref, sin_ref, cosq_ref, sinq_ref,
                qlg_ref, wuq_ref, kvlg_ref, wukv_ref, qgn_ref, qga_ref, qgb_ref,
                kgn_ref, kga_ref, kgb_ref, gn_ref, o_ref, kf_ref, vt_ref, *flash):
    i = pl.program_id(1)
    scale = MLA_QK_DIM ** -0.5 * LOG2E
    chunk = 512

    @pl.when(i == 0)
    def _prep():
        for c in range(SEQ // chunk):
            rows = slice(c * chunk, (c + 1) * chunk)
            ckv = (_rms(ckv_ref[rows, :].astype(F32)) * kvlg_ref[...]).astype(BF16)
            kv = _dot(ckv, wukv_ref[...])
            kr = kr_ref[rows, :].astype(F32)
            kpe = _rope128(kr[:, :128], kr[:, 128:], kga_ref[...], kgb_ref[...],
                           cos_ref[rows, :], sin_ref[rows, :]).astype(BF16)
            for h in range(N_HEADS):
                sl = slice(h * HEAD_DIM, (h + 1) * HEAD_DIM)
                kf_ref[h, rows, 0:128] = (_rms(kv[:, sl]) * kgn_ref[...]).astype(BF16)
                kf_ref[h, rows, 128:256] = kpe
                _store_vt(vt_ref, h, c * (chunk // TK),
                          kv[:, GROUP_WIDTH + h * HEAD_DIM:GROUP_WIDTH + (h + 1) * HEAD_DIM])

    cq = (_rms(cq_ref[...].astype(F32)) * qlg_ref[...]).astype(BF16)
    qraw = _dot(cq, wuq_ref[...])
    qf = []
    for h in range(N_HEADS):
        base = h * MLA_Q_HEAD_COLS
        qn = _rms(qraw[:, base:base + 128]) * qgn_ref[...]
        qpe = _rope128(qraw[:, base + 128:base + 256], qraw[:, base + 256:base + 384],
                       qga_ref[...], qgb_ref[...], cosq_ref[...], sinq_ref[...])
        qf.append((jnp.concatenate([qn, qpe], axis=-1) * scale).astype(BF16))

    def kv(h, j):
        off = pl.multiple_of(j * TK, TK)
        return kf_ref[h, pl.ds(off, TK), :], vt_ref[j, h * VT_ROWS:(h + 1) * VT_ROWS, :], None

    _group_out(_flash_heads(i, qf, kv, flash), gn_ref, o_ref)


def _mla(proj, cos, sin, qlg, wuq, kvlg, wukv, qgn, qga, qgb, kgn, kga, kgb, gn):
    full = lambda a: pl.BlockSpec(a.shape, lambda b, i: (0,) * a.ndim)
    return pl.pallas_call(
        _mla_kernel,
        grid=(BATCH, NQ),
        in_specs=[pl.BlockSpec((TQ, MLA_Q_RANK), lambda b, i: (b * NQ + i, COL_MCQ // MLA_Q_RANK)),
                  pl.BlockSpec((SEQ, MLA_KV_RANK), lambda b, i: (b, COL_MCKV // MLA_KV_RANK)),
                  pl.BlockSpec((SEQ, 256), lambda b, i: (b, COL_KROPE // 256)),
                  full(cos), full(sin),
                  pl.BlockSpec((TQ, 128), lambda b, i: (i, 0)),
                  pl.BlockSpec((TQ, 128), lambda b, i: (i, 0)),
                  full(qlg), full(wuq), full(kvlg), full(wukv), full(qgn), full(qga), full(qgb),
                  full(kgn), full(kga), full(kgb), full(gn)],
        out_specs=pl.BlockSpec((TQ, GROUP_WIDTH), lambda b, i: (b * NQ + i, 0)),
        out_shape=jax.ShapeDtypeStruct((TOKENS, GROUP_WIDTH), BF16),
        scratch_shapes=[pltpu.VMEM((N_HEADS, SEQ, MLA_K_COLS), BF16),
                        pltpu.VMEM((NKV, N_HEADS * VT_ROWS, TK), BF16)] + _flash_scratch(),
        compiler_params=_cparams(("arbitrary", "arbitrary")),
        name="mla",
    )(proj, proj, proj, cos, sin, cos, sin, qlg, wuq, kvlg, wukv, qgn, qga, qgb, kgn, kga, kgb, gn)


def _gelu(x):
    return 0.5 * x * (1.0 + jnp.tanh(0.7978845608028654 * (x + 0.044715 * (x * x * x))))


def _gmlp_kernel(u_ref, v_ref, lg_ref, lb_ref, ws_ref, bs_ref, gn_ref, o_ref, mix_ref):
    v = _gelu(v_ref[...].astype(F32))
    mu = jnp.mean(v, axis=-1, keepdims=True)
    vc = v - mu
    var = jnp.mean(vc * vc, axis=-1, keepdims=True)
    vn = (vc * lax.rsqrt(var + EPS) * lg_ref[...] + lb_ref[...]).astype(BF16)
    r = lax.broadcasted_iota(jnp.int32, (GMLP_CHUNK, GMLP_CHUNK), 0)
    c = lax.broadcasted_iota(jnp.int32, (GMLP_CHUNK, GMLP_CHUNK), 1)
    for g in range(GROUP_WIDTH // HEAD_DIM):
        w = jnp.where(c <= r, ws_ref[g], 0.0).astype(BF16)
        cols = slice(g * HEAD_DIM, (g + 1) * HEAD_DIM)
        for n in range(ROW_TILE // GMLP_CHUNK):
            rows = slice(n * GMLP_CHUNK, (n + 1) * GMLP_CHUNK)
            mix_ref[rows, cols] = _dot(w, vn[rows, cols]) + bs_ref[:, cols]
    o = _gelu(u_ref[...].astype(F32)) * mix_ref[...]
    o_ref[...] = (_rms(o) * gn_ref[...]).astype(o_ref.dtype)


def _gmlp(proj, lg, lb, ws, bs, gn):
    full = lambda a: pl.BlockSpec(a.shape, lambda m: (0,) * a.ndim)
    return pl.pallas_call(
        _gmlp_kernel,
        grid=(TOKENS // ROW_TILE,),
        in_specs=[pl.BlockSpec((ROW_TILE, GROUP_WIDTH), lambda m: (m, COL_GU // GROUP_WIDTH)),
                  pl.BlockSpec((ROW_TILE, GROUP_WIDTH), lambda m: (m, COL_GV // GROUP_WIDTH)),
                  full(lg), full(lb), full(ws), full(bs), full(gn)],
        out_specs=pl.BlockSpec((ROW_TILE, GROUP_WIDTH), lambda m: (m, 0)),
        out_shape=jax.ShapeDtypeStruct((TOKENS, GROUP_WIDTH), BF16),
        scratch_shapes=[pltpu.VMEM((ROW_TILE, GROUP_WIDTH), F32)],
        compiler_params=_cparams(("arbitrary",)),
        name="gmlp",
    )(proj, proj, lg, lb, ws, bs, gn)


def _moba_kernel(q_ref, k_ref, v_ref, qg_ref, kg_ref, gn_ref, o_ref, kn_ref, vt_ref, km_ref,
                 sel_ref, *flash):
    i = pl.program_id(1)
    scale = HEAD_DIM ** -0.5 * LOG2E

    @pl.when(i == 0)
    def _prep():
        for h in range(N_HEADS):
            sl = slice(h * HEAD_DIM, (h + 1) * HEAD_DIM)
            means = []
            for n in range(N_MOBA_BLOCKS):
                rows = slice(n * MOBA_BLOCK, (n + 1) * MOBA_BLOCK)
                kn = _rms(k_ref[rows, sl].astype(F32)) * kg_ref[...]
                kn_ref[rows, sl] = kn.astype(BF16)
                means.append(jnp.mean(kn, axis=0, keepdims=True))
            km_ref[h] = jnp.concatenate(means, axis=0)
            _store_vt(vt_ref, h, 0, v_ref[:, sl].astype(F32))

    blk = lax.broadcasted_iota(jnp.int32, (N_MOBA_BLOCKS, TQ), 0)
    sls = [slice(h * HEAD_DIM, (h + 1) * HEAD_DIM) for h in range(N_HEADS)]
    qn = []
    for h in range(N_HEADS):
        q32 = _rms(q_ref[:, sls[h]].astype(F32)) * qg_ref[...]
        qn.append((q32 * scale).astype(BF16))

        gate = _dot_nt(km_ref[h], q32, precision=lax.Precision.HIGHEST)
        gate = jnp.where(blk < i, gate, NEG)
        sel_t = jnp.zeros((N_MOBA_BLOCKS, TQ), F32)
        for n in range(N_MOBA_BLOCKS):
            gn_row = gate[n:n + 1, :]
            beats = jnp.where(gate > gn_row, 1.0, jnp.where((gate == gn_row) & (blk < n), 1.0, 0.0))
            rank = jnp.sum(beats, axis=0, keepdims=True)
            past = jnp.full((1, TQ), n, jnp.int32) < i
            chosen = jnp.where(rank < MOBA_TOPK, jnp.where(past, 1.0, 0.0), 0.0)
            sel_t = jnp.where(blk == n, chosen, sel_t)
        sel_ref[h] = sel_t

    def kv(h, j):
        off = pl.multiple_of(j * TK, TK)
        return kn_ref[pl.ds(off, TK), sls[h]], vt_ref[j, h * VT_ROWS:(h + 1) * VT_ROWS, :], None

    def keep(h, j):
        return sel_ref[h, pl.ds(j, 1), :]

    _group_out(_flash_heads(i, qn, kv, flash, keep), gn_ref, o_ref)


def _moba(proj, qg, kg, gn):
    row = lambda w: pl.BlockSpec((1, w), lambda b, i: (0, 0))
    return pl.pallas_call(
        _moba_kernel,
        grid=(BATCH, NQ),
        in_specs=[pl.BlockSpec((TQ, GROUP_WIDTH), lambda b, i: (b * NQ + i, COL_BQ // GROUP_WIDTH)),
                  pl.BlockSpec((SEQ, GROUP_WIDTH), lambda b, i: (b, COL_BK // GROUP_WIDTH)),
                  pl.BlockSpec((SEQ, GROUP_WIDTH), lambda b, i: (b, COL_BV // GROUP_WIDTH)),
                  row(HEAD_DIM), row(HEAD_DIM), row(GROUP_WIDTH)],
        out_specs=pl.BlockSpec((TQ, GROUP_WIDTH), lambda b, i: (b * NQ + i, 0)),
        out_shape=jax.ShapeDtypeStruct((TOKENS, GROUP_WIDTH), BF16),
        scratch_shapes=[pltpu.VMEM((SEQ, GROUP_WIDTH), BF16),
                        pltpu.VMEM((NKV, N_HEADS * VT_ROWS, TK), BF16),
                        pltpu.VMEM((N_HEADS, N_MOBA_BLOCKS, HEAD_DIM), F32),
                        pltpu.VMEM((N_HEADS, N_MOBA_BLOCKS, TQ), F32)] + _flash_scratch(),
        compiler_params=_cparams(("arbitrary", "arbitrary")),
        name="moba",
    )(proj, proj, proj, qg, kg, gn)


def _outproj_kernel(x_ref, a_ref, b_ref, c_ref, d_ref, w_ref, o_ref):
    mixed = jnp.concatenate([a_ref[...], b_ref[...], c_ref[...], d_ref[...]], axis=-1)
    o_ref[...] = x_ref[...] + _dot(mixed, w_ref[...])


def _outproj(x, oa, ob, oc, od, w):
    grp = pl.BlockSpec((ROW_TILE, GROUP_WIDTH), lambda m: (m, 0))
    return pl.pallas_call(
        _outproj_kernel,
        grid=(TOKENS // ROW_TILE,),
        in_specs=[pl.BlockSpec((ROW_TILE, D_MODEL), lambda m: (m, 0)), grp, grp, grp, grp,
                  pl.BlockSpec((D_MODEL, D_MODEL), lambda m: (0, 0))],
        out_specs=pl.BlockSpec((ROW_TILE, D_MODEL), lambda m: (m, 0)),
        out_shape=jax.ShapeDtypeStruct((TOKENS, D_MODEL), F32),
        compiler_params=_cparams(("arbitrary",)),
        name="outproj",
    )(x, oa, ob, oc, od, w)


def _memkv_kernel(m_ref, g_ref, w_ref, kg_ref, k_ref, v_ref):
    h = (_rms(m_ref[...]) * g_ref[...]).astype(BF16)
    kv = _dot(h, w_ref[...])
    for hd in range(N_HEADS):
        sl = slice(hd * HEAD_DIM, (hd + 1) * HEAD_DIM)
        k_ref[:, sl] = (_rms(kv[:, sl]) * kg_ref[...]).astype(BF16)
    v_ref[...] = kv[:, GROUP_WIDTH:].astype(BF16)


def _memkv(mem, g, w, kg):
    out = pl.BlockSpec((MEM_LEN, GROUP_WIDTH), lambda b: (b, 0))
    return pl.pallas_call(
        _memkv_kernel,
        grid=(BATCH,),
        in_specs=[pl.BlockSpec((MEM_LEN, D_MODEL), lambda b: (b, 0)),
                  pl.BlockSpec((1, D_MODEL), lambda b: (0, 0)),
                  pl.BlockSpec((D_MODEL, 2 * GROUP_WIDTH), lambda b: (0, 0)),
                  pl.BlockSpec((1, HEAD_DIM), lambda b: (0, 0))],
        out_specs=[out, out],
        out_shape=[jax.ShapeDtypeStruct((BATCH * MEM_LEN, GROUP_WIDTH), BF16)] * 2,
        compiler_params=_cparams(("arbitrary",)),
        name="memkv",
    )(mem, g, w, kg)


def _xattn_kernel(x_ref, g_ref, wq_ref, qg_ref, k_ref, v_ref, wo_ref, o_ref):
    x = x_ref[...]
    h = (_rms(x) * g_ref[...]).astype(BF16)
    q = _dot(h, wq_ref[...])
    scale = HEAD_DIM ** -0.5 * LOG2E
    heads = []
    for hd in range(N_HEADS):
        sl = slice(hd * HEAD_DIM, (hd + 1) * HEAD_DIM)
        qn = (_rms(q[:, sl]) * qg_ref[...] * scale).astype(BF16)
        s = _dot_nt(qn, k_ref[:, sl])
        p = jnp.exp2(s - jnp.max(s, axis=-1, keepdims=True))
        o = _dot(p.astype(BF16), v_ref[:, sl])
        heads.append(o / jnp.sum(p, axis=-1, keepdims=True))
    a = jnp.concatenate(heads, axis=-1).astype(BF16)
    o_ref[...] = x + _dot(a, wo_ref[...])


def _xattn(x, g, wq, qg, k, v, wo):
    per_seq = SEQ // ROW_TILE
    return pl.pallas_call(
        _xattn_kernel,
        grid=(TOKENS // ROW_TILE,),
        in_specs=[pl.BlockSpec((ROW_TILE, D_MODEL), lambda m: (m, 0)),
                  pl.BlockSpec((1, D_MODEL), lambda m: (0, 0)),
                  pl.BlockSpec((D_MODEL, GROUP_WIDTH), lambda m: (0, 0)),
                  pl.BlockSpec((1, HEAD_DIM), lambda m: (0, 0)),
                  pl.BlockSpec((MEM_LEN, GROUP_WIDTH), lambda m: (m // per_seq, 0)),
                  pl.BlockSpec((MEM_LEN, GROUP_WIDTH), lambda m: (m // per_seq, 0)),
                  pl.BlockSpec((GROUP_WIDTH, D_MODEL), lambda m: (0, 0))],
        out_specs=pl.BlockSpec((ROW_TILE, D_MODEL), lambda m: (m, 0)),
        out_shape=jax.ShapeDtypeStruct((TOKENS, D_MODEL), F32),
        compiler_params=_cparams(("arbitrary",)),
        name="xattn",
    )(x, g, wq, qg, k, v, wo)


def _ffn_kernel(x_ref, g_ref, wg_ref, wu_ref, wd_ref, o_ref, h_ref):
    @pl.when(pl.program_id(1) == 0)
    def _():
        x = x_ref[...]
        h_ref[...] = (_rms(x) * g_ref[...]).astype(BF16)
        o_ref[...] = x

    h = h_ref[...]
    gate = _dot(h, wg_ref[...].astype(BF16))
    up = _dot(h, wu_ref[...].astype(BF16))
    a = (gate * (1.0 / (1.0 + jnp.exp(-gate))) * up).astype(BF16)
    o_ref[...] += _dot(a, wd_ref[...].astype(BF16))


def _ffn(x, g, wgu, wd, l):
    bm, bf = FFN_BM, FFN_BF
    nf = D_FF // bf
    return pl.pallas_call(
        _ffn_kernel,
        grid=(TOKENS // bm, nf),
        in_specs=[pl.BlockSpec((bm, D_MODEL), lambda m, f: (m, 0)),
                  pl.BlockSpec((1, D_MODEL), lambda m, f: (0, 0)),
                  pl.BlockSpec((None, D_MODEL, bf), lambda m, f: (l, 0, f)),
                  pl.BlockSpec((None, D_MODEL, bf), lambda m, f: (l, 0, nf + f)),
                  pl.BlockSpec((None, bf, D_MODEL), lambda m, f: (l, f, 0))],
        out_specs=pl.BlockSpec((bm, D_MODEL), lambda m, f: (m, 0)),
        out_shape=jax.ShapeDtypeStruct((TOKENS, D_MODEL), F32),
        scratch_shapes=[pltpu.VMEM((bm, D_MODEL), BF16)],
        compiler_params=_cparams(("arbitrary", "arbitrary")),
        name="ffn",
    )(x, g, wgu, wgu, wd)


def _rot_cols(w):
    half = MLA_ROPE_DIM // 2
    return jnp.concatenate([-w[..., half:], w[..., :half]], axis=-1)


def _swap_halves(g):
    half = MLA_ROPE_DIM // 2
    return jnp.concatenate([g[half:], g[:half]])


def _rope_tables():
    pos = jnp.arange(SEQ, dtype=F32)
    inv_freq = ROPE_THETA ** (-jnp.arange(0, MLA_ROPE_DIM, 2, dtype=F32) / MLA_ROPE_DIM)
    ang = pos[:, None] * inv_freq[None, :]
    zeros = jnp.zeros((SEQ, MLA_ROPE_DIM), F32)
    cos = jnp.concatenate([jnp.cos(ang), jnp.cos(ang), zeros], axis=-1)
    sin = jnp.concatenate([jnp.sin(ang), jnp.sin(ang), zeros], axis=-1)
    return cos, sin


def _row(v):
    return v.reshape(1, -1).astype(F32)


def _pad_lanes(v, width):
    return jnp.pad(v.astype(F32), (0, width - v.shape[0])).reshape(1, width)


BF16_WEIGHTS = ('w_in', 'mla_w_uq', 'mla_w_ukv', 'w_out', 'w_xq', 'w_xkv', 'w_xo')


def _layer_params(l, p):
    w_in = p['w_in'][l]
    off = [0]
    for n in (512, 512, 512, 4, 512, 256, 64, 512, 512, 512, 512, 512):
        off.append(off[-1] + n)
    seg = lambda k: w_in[:, off[k]:off[k + 1]]
    fq, fk, fv, ff, mcq, mckv, mkr, gu, gv, bq, bk, bv = (seg(k) for k in range(12))
    mkr_rot = _rot_cols(mkr)
    w_main = jnp.concatenate([fq, fk, fv, mcq, gu, gv, bq, bk, bv, mckv,
                              mkr, mkr, mkr_rot, mkr_rot], axis=1).astype(BF16)
    w_small = jnp.pad(ff, ((0, 0), (0, SMALL_COLS - N_HEADS))).astype(BF16)

    wuq = p['mla_w_uq'][l].reshape(MLA_Q_RANK, N_HEADS, MLA_QK_DIM)
    nope, pe = wuq[..., :MLA_NOPE_DIM], wuq[..., MLA_NOPE_DIM:]
    rot = _rot_cols(pe)
    wuq = jnp.concatenate([nope, pe, pe, rot, rot], axis=-1).reshape(
        MLA_Q_RANK, N_HEADS * MLA_Q_HEAD_COLS).astype(BF16)
    wukv = p['mla_w_ukv'][l].reshape(MLA_KV_RANK, N_HEADS, 2 * HEAD_DIM)
    wukv = jnp.concatenate([wukv[..., :HEAD_DIM].reshape(MLA_KV_RANK, -1),
                            wukv[..., HEAD_DIM:].reshape(MLA_KV_RANK, -1)], axis=1).astype(BF16)
    qg, kg = p['mla_q_norm'][l], p['mla_k_norm'][l]
    two = lambda g: _row(jnp.concatenate([g, g]))

    bs = jnp.repeat(p['gmlp_b_s'][l].T, HEAD_DIM, axis=1).astype(F32)

    return dict(
        mix_norm=_row(p['mix_norm'][l]), w_main=w_main, w_small=w_small,
        fox_b_f=_pad_lanes(p['fox_b_f'][l], SMALL_COLS),
        fox_q=_row(p['fox_q_norm'][l]), fox_k=_row(p['fox_k_norm'][l]),
        mla_qlg=_row(p['mla_q_lora_norm'][l]), mla_wuq=wuq,
        mla_kvlg=_row(p['mla_kv_lora_norm'][l]), mla_wukv=wukv,
        mla_qgn=_row(qg[:MLA_NOPE_DIM]), mla_qga=two(qg[MLA_NOPE_DIM:]),
        mla_qgb=two(_swap_halves(qg[MLA_NOPE_DIM:])),
        mla_kgn=_row(kg[:MLA_NOPE_DIM]), mla_kga=two(kg[MLA_NOPE_DIM:]),
        mla_kgb=two(_swap_halves(kg[MLA_NOPE_DIM:])),
        gmlp_lg=_row(p['gmlp_ln_g'][l]), gmlp_lb=_row(p['gmlp_ln_b'][l]),
        gmlp_ws=p['gmlp_w_s'][l].astype(F32), gmlp_bs=bs,
        moba_q=_row(p['moba_q_norm'][l]), moba_k=_row(p['moba_k_norm'][l]),
        gn=[_row(p['group_norm'][l, g]) for g in range(4)],
        w_out=p['w_out'][l].astype(BF16),
        xattn_norm=_row(p['xattn_norm'][l]), mem_norm=_row(p['mem_norm'][l]),
        w_xq=p['w_xq'][l].astype(BF16), w_xkv=p['w_xkv'][l].astype(BF16),
        xq=_row(p['xattn_q_norm'][l]), xk=_row(p['xattn_k_norm'][l]),
        w_xo=p['w_xo'][l].astype(BF16),
        ffn_norm=_row(p['ffn_norm'][l]),
    )


def kernel(x, mem, mix_norm, w_in, fox_b_f, fox_q_norm, fox_k_norm, mla_q_lora_norm, mla_w_uq,
           mla_kv_lora_norm, mla_w_ukv, mla_q_norm, mla_k_norm, gmlp_ln_g, gmlp_ln_b, gmlp_w_s,
           gmlp_b_s, moba_q_norm, moba_k_norm, group_norm, w_out, xattn_norm, mem_norm, w_xq, w_xkv,
           xattn_q_norm, xattn_k_norm, w_xo, ffn_norm, w_gate_up, w_down):
    params = dict(mix_norm=mix_norm, w_in=w_in, fox_b_f=fox_b_f, fox_q_norm=fox_q_norm,
                  fox_k_norm=fox_k_norm, mla_q_lora_norm=mla_q_lora_norm, mla_w_uq=mla_w_uq,
                  mla_kv_lora_norm=mla_kv_lora_norm, mla_w_ukv=mla_w_ukv, mla_q_norm=mla_q_norm,
                  mla_k_norm=mla_k_norm, gmlp_ln_g=gmlp_ln_g, gmlp_ln_b=gmlp_ln_b,
                  gmlp_w_s=gmlp_w_s, gmlp_b_s=gmlp_b_s, moba_q_norm=moba_q_norm,
                  moba_k_norm=moba_k_norm, group_norm=group_norm, w_out=w_out,
                  xattn_norm=xattn_norm, mem_norm=mem_norm, w_xq=w_xq, w_xkv=w_xkv,
                  xattn_q_norm=xattn_q_norm, xattn_k_norm=xattn_k_norm, w_xo=w_xo,
                  ffn_norm=ffn_norm, w_gate_up=w_gate_up, w_down=w_down)
    for name in BF16_WEIGHTS:
        params[name] = params[name].astype(BF16)
    cos, sin = _rope_tables()
    xf = x.reshape(TOKENS, D_MODEL)
    memf = mem.reshape(BATCH * MEM_LEN, D_MODEL)
    for l in range(DEPTH):
        q = _layer_params(l, params)
        proj, small = _inproj(xf, q['mix_norm'], q['w_main'], q['w_small'])
        o_a = _fox(proj, small, q['fox_b_f'], q['fox_q'], q['fox_k'], q['gn'][0])
        o_b = _mla(proj, cos, sin, q['mla_qlg'], q['mla_wuq'], q['mla_kvlg'], q['mla_wukv'],
                   q['mla_qgn'], q['mla_qga'], q['mla_qgb'], q['mla_kgn'], q['mla_kga'],
                   q['mla_kgb'], q['gn'][1])
        o_c = _gmlp(proj, q['gmlp_lg'], q['gmlp_lb'], q['gmlp_ws'], q['gmlp_bs'], q['gn'][2])
        o_d = _moba(proj, q['moba_q'], q['moba_k'], q['gn'][3])
        xf = _outproj(xf, o_a, o_b, o_c, o_d, q['w_out'])
        mk, mv = _memkv(memf, q['mem_norm'], q['w_xkv'], q['xk'])
        xf = _xattn(xf, q['xattn_norm'], q['w_xq'], q['xq'], mk, mv, q['w_xo'])
        xf = _ffn(xf, q['ffn_norm'], w_gate_up, w_down, l)
    return xf.reshape(BATCH, SEQ, D_MODEL)
```

```python
from typing import NamedTuple

import jax
import jax.numpy as jnp
from jax import lax
from jax.experimental import pallas as pl
from jax.experimental.pallas import tpu as pltpu

F32 = jnp.float32
BF16 = jnp.bfloat16

D_MODEL = 2048
BATCH = 8
SEQ = 2048
DEPTH = 2
TOKENS = BATCH * SEQ
MEM_LEN = 256
GROUP_WIDTH = 512
HEAD_DIM = 128
N_HEADS = 4
MLA_Q_RANK = 512
MLA_KV_RANK = 256
MLA_NOPE_DIM = 128
MLA_ROPE_DIM = 64
MLA_QK_DIM = MLA_NOPE_DIM + MLA_ROPE_DIM
ROPE_THETA = 10000.0
GMLP_CHUNK = 128
MOBA_BLOCK = 256
MOBA_TOPK = 3
N_MOBA_BLOCKS = SEQ // MOBA_BLOCK
D_FF = 5632
EPS = 1e-6
NEG = -1e30
LOG2E = 1.4426950408889634

V7X_VMEM_LIMIT_BYTES = 56 * 1024 * 1024

COL_FQ, COL_FK, COL_FV = 0, 512, 1024
COL_MCQ = 1536
COL_GU, COL_GV = 2048, 2560
COL_BQ, COL_BK, COL_BV = 3072, 3584, 4096
COL_MCKV = 4608
COL_KROPE = 4864
PROJ_COLS = 5120
SMALL_COLS = 128

TQ = 256
TK = 256
NQ = SEQ // TQ
NKV = SEQ // TK
INPROJ_BM, INPROJ_BN = 1024, 1024
ROW_TILE = 512
FFN_BM, FFN_BF = 1024, 256


def _cparams(sem):
    return pltpu.CompilerParams(dimension_semantics=sem,
                                vmem_limit_bytes=V7X_VMEM_LIMIT_BYTES)


class _Layer(NamedTuple):
    stacked: jax.Array
    index: int


def _whole(p):
    if isinstance(p, _Layer):
        shape = p.stacked.shape
        return pl.BlockSpec((None,) + shape[1:], lambda *_: (p.index,) + (0,) * (len(shape) - 1))
    return pl.BlockSpec(p.shape, lambda *_: (0,) * p.ndim)


def _arr(p):
    return p.stacked if isinstance(p, _Layer) else p


def _rms(x):
    return x * lax.rsqrt(jnp.mean(x * x, axis=-1, keepdims=True) + EPS)


def _dot(a, b):
    return jnp.dot(a, b, preferred_element_type=F32)


def _dot_nt(a, b, precision=None):
    return lax.dot_general(a, b, (((1,), (1,)), ((), ())),
                           preferred_element_type=F32, precision=precision)


VT_ROWS = HEAD_DIM + 16


def _flash_scratch():
    return [pltpu.VMEM((2, N_HEADS, TK, TQ), F32), pltpu.VMEM((2, N_HEADS, 1, TQ), F32),
            pltpu.VMEM((N_HEADS, 1, TQ), F32), pltpu.VMEM((N_HEADS, VT_ROWS, TQ), F32)]


def _flash_heads(i, q, kv, scratch, keep=None):
    s_ref, cm_ref, m_ref, acc_ref = scratch
    key_idx = lax.broadcasted_iota(jnp.int32, (TK, TQ), 0)
    qry_idx = lax.broadcasted_iota(jnp.int32, (TK, TQ), 1)

    def stage_a(j, slot, diag=False):
        raw = [_dot_nt(kv(h, j)[0], q[h]) for h in range(N_HEADS)]
        for h in range(N_HEADS):
            s, bias = raw[h], kv(h, j)[2]
            if bias is not None:
                s = s - bias
            if diag:
                s = jnp.where(key_idx <= qry_idx, s, NEG)
            elif keep is not None:
                s = jnp.where(keep(h, j) > 0.5, s, NEG)
            s_ref[slot, h] = s
            cm_ref[slot, h] = jnp.max(s, axis=0, keepdims=True)

    def stage_b(j, slot, gate=None):
        probs, alphas = [], []
        for h in range(N_HEADS):
            m_old = m_ref[h]
            m_new = jnp.maximum(m_old, cm_ref[slot, h])
            m_ref[h] = m_new
            probs.append(jnp.exp2(s_ref[slot, h] - m_new).astype(BF16))
            alphas.append(jnp.exp2(m_old - m_new))
        for h in range(N_HEADS):
            pv = _dot(kv(h, j)[1], probs[h])
            if gate is not None:
                pv = pv * gate
            acc_ref[h] = alphas[h] * acc_ref[h] + pv

    m_ref[...] = jnp.full(m_ref.shape, NEG, F32)
    acc_ref[...] = jnp.zeros(acc_ref.shape, F32)
    last = jnp.maximum(i - 1, 0)
    stage_a(i, 0, diag=True)
    stage_a(0, 1)
    stage_b(i, 0)

    def body(p, _):
        t0 = 2 * p
        t1 = jnp.minimum(t0 + 1, last)
        stage_a(t1, 0)
        stage_b(t0, 1)
        stage_a(jnp.minimum(t0 + 2, last), 1)
        stage_b(t1, 0, gate=(t0 + 1 < i).astype(F32))
        return 0

    lax.fori_loop(0, (i + 1) // 2, body, 0)
    return [(acc_ref[h, :HEAD_DIM] / acc_ref[h, HEAD_DIM:HEAD_DIM + 1]).T for h in range(N_HEADS)]


def _store_vt(vt_ref, h, first_tile, v32):
    base = h * VT_ROWS
    for t in range(v32.shape[0] // TK):
        vt_ref[first_tile + t, base:base + HEAD_DIM, :] = v32[t * TK:(t + 1) * TK].T.astype(BF16)
        vt_ref[first_tile + t, base + HEAD_DIM:base + VT_ROWS, :] = jnp.ones((VT_ROWS - HEAD_DIM, TK), BF16)


def _group_out(heads, gn_ref, o_ref):
    o = jnp.concatenate(heads, axis=-1)
    o_ref[...] = (_rms(o) * gn_ref[...]).astype(o_ref.dtype)


def _inproj_kernel(x_ref, g_ref, w_ref, ws_ref, o_ref, os_ref, h_ref):
    @pl.when(pl.program_id(1) == 0)
    def _():
        h = (_rms(x_ref[...]) * g_ref[...]).astype(BF16)
        h_ref[...] = h
        os_ref[...] = _dot(h, ws_ref[...])

    o_ref[...] = _dot(h_ref[...], w_ref[...]).astype(o_ref.dtype)


def _inproj(x, g, w, ws):
    bm, bn = INPROJ_BM, INPROJ_BN
    return pl.pallas_call(
        _inproj_kernel,
        grid=(TOKENS // bm, PROJ_COLS // bn),
        in_specs=[pl.BlockSpec((bm, D_MODEL), lambda m, n: (m, 0)),
                  _whole(g),
                  pl.BlockSpec((D_MODEL, bn), lambda m, n: (0, n)),
                  pl.BlockSpec((D_MODEL, SMALL_COLS), lambda m, n: (0, 0))],
        out_specs=[pl.BlockSpec((bm, bn), lambda m, n: (m, n)),
                   pl.BlockSpec((bm, SMALL_COLS), lambda m, n: (m, 0))],
        out_shape=[jax.ShapeDtypeStruct((TOKENS, PROJ_COLS), BF16),
                   jax.ShapeDtypeStruct((TOKENS, SMALL_COLS), F32)],
        scratch_shapes=[pltpu.VMEM((bm, D_MODEL), BF16)],
        compiler_params=_cparams(("arbitrary", "arbitrary")),
        name="inproj",
    )(x, _arr(g), w, ws)


IN_SPLITS = (512, 512, 512, 4, 512, 256, 64, 512, 512, 512, 512, 512)
IN_COLS = sum(IN_SPLITS)
RELAYOUT_ROWS = 256


def _rot_cols(w):
    half = MLA_ROPE_DIM // 2
    return jnp.concatenate([-w[..., half:], w[..., :half]], axis=-1)


def _relayout_kernel(w_ref, o_ref, s_ref):
    w = w_ref[...]
    off = [0]
    for n in IN_SPLITS:
        off.append(off[-1] + n)
    fq, fk, fv, ff, mcq, mckv, mkr, gu, gv, bq, bk, bv = (w[:, off[k]:off[k + 1]] for k in range(12))
    rot = _rot_cols(mkr)
    o_ref[...] = jnp.concatenate([fq, fk, fv, mcq, gu, gv, bq, bk, bv, mckv, mkr, mkr, rot, rot],
                                 axis=1).astype(BF16)
    pad = jnp.zeros((w.shape[0], SMALL_COLS - N_HEADS), F32)
    s_ref[...] = jnp.concatenate([ff, pad], axis=1).astype(BF16)


def _relayout_w_in(w_in, l):
    rows = RELAYOUT_ROWS
    return pl.pallas_call(
        _relayout_kernel,
        grid=(D_MODEL // rows,),
        in_specs=[pl.BlockSpec((None, rows, IN_COLS), lambda m: (l, m, 0))],
        out_specs=[pl.BlockSpec((rows, PROJ_COLS), lambda m: (m, 0)),
                   pl.BlockSpec((rows, SMALL_COLS), lambda m: (m, 0))],
        out_shape=[jax.ShapeDtypeStruct((D_MODEL, PROJ_COLS), BF16),
                   jax.ShapeDtypeStruct((D_MODEL, SMALL_COLS), BF16)],
        compiler_params=_cparams(("arbitrary",)),
        name="relayout_w_in",
    )(w_in)


def _fox_kernel(q_ref, k_ref, v_ref, f_ref, bf_ref, qg_ref, kg_ref, gn_ref, o_ref,
                kn_ref, vt_ref, c_ref, *flash):
    i = pl.program_id(1)
    scale = HEAD_DIM ** -0.5 * LOG2E
    sls = [slice(h * HEAD_DIM, (h + 1) * HEAD_DIM) for h in range(N_HEADS)]

    @pl.when(i == 0)
    def _prep():
        x = f_ref[...] + bf_ref[...]
        c = jnp.minimum(x, 0.0) - jnp.log1p(jnp.exp(-jnp.abs(x)))
        row = lax.broadcasted_iota(jnp.int32, c.shape, 0)
        step = 1
        while step < SEQ:
            c = c + jnp.where(row >= step, pltpu.roll(c, step, 0), 0.0)
            step *= 2
        for h in range(N_HEADS):
            kn_ref[:, sls[h]] = (_rms(k_ref[:, sls[h]].astype(F32)) * kg_ref[...]).astype(BF16)
            _store_vt(vt_ref, h, 0, v_ref[:, sls[h]].astype(F32))
            c_ref[h] = jnp.broadcast_to(c[:, h:h + 1] * LOG2E, (SEQ, HEAD_DIM))

    q = [(_rms(q_ref[:, sl].astype(F32)) * qg_ref[...] * scale).astype(BF16) for sl in sls]

    def kv(h, j):
        off = pl.multiple_of(j * TK, TK)
        c = c_ref[h, pl.ds(off, TK), :]
        return (kn_ref[pl.ds(off, TK), sls[h]],
                vt_ref[j, h * VT_ROWS:(h + 1) * VT_ROWS, :],
                jnp.concatenate([c] * (TQ // HEAD_DIM), axis=1))

    _group_out(_flash_heads(i, q, kv, flash), gn_ref, o_ref)


def _fox(proj, small, b_f, qg, kg, gn):
    return pl.pallas_call(
        _fox_kernel,
        grid=(BATCH, NQ),
        in_specs=[pl.BlockSpec((TQ, GROUP_WIDTH), lambda b, i: (b * NQ + i, COL_FQ // GROUP_WIDTH)),
                  pl.BlockSpec((SEQ, GROUP_WIDTH), lambda b, i: (b, COL_FK // GROUP_WIDTH)),
                  pl.BlockSpec((SEQ, GROUP_WIDTH), lambda b, i: (b, COL_FV // GROUP_WIDTH)),
                  pl.BlockSpec((SEQ, SMALL_COLS), lambda b, i: (b, 0)),
                  _whole(b_f), _whole(qg), _whole(kg), _whole(gn)],
        out_specs=pl.BlockSpec((TQ, GROUP_WIDTH), lambda b, i: (b * NQ + i, 0)),
        out_shape=jax.ShapeDtypeStruct((TOKENS, GROUP_WIDTH), BF16),
        scratch_shapes=[pltpu.VMEM((SEQ, GROUP_WIDTH), BF16),
                        pltpu.VMEM((NKV, N_HEADS * VT_ROWS, TK), BF16),
                        pltpu.VMEM((N_HEADS, SEQ, HEAD_DIM), F32)] + _flash_scratch(),
        compiler_params=_cparams(("arbitrary", "arbitrary")),
        name="fox",
    )(proj, proj, proj, small, _arr(b_f), _arr(qg), _arr(kg), _arr(gn))


MLA_Q_HEAD_COLS = 384
MLA_K_COLS = 256


def _rope128(a, b, ga, gb, cos, sin):
    r = lax.rsqrt(jnp.sum(a * a, axis=-1, keepdims=True) * (1.0 / (2 * MLA_ROPE_DIM)) + EPS)
    return r * (a * ga * cos + b * gb * sin)


def _mla_kernel(cq_ref, ckv_ref, kr_ref, cos_ref, sin_ref, cosq_ref, sinq_ref,
                qlg_ref, wuq_ref, kvlg_ref, wukv_ref, qgn_ref, qga_ref, qgb_ref,
                kgn_ref, kga_ref, kgb_ref, gn_ref, o_ref, kf_ref, vt_ref, *flash):
    i = pl.program_id(1)
    scale = MLA_QK_DIM ** -0.5 * LOG2E
    chunk = 512

    @pl.when(i == 0)
    def _prep():
        for c in range(SEQ // chunk):
            rows = slice(c * chunk, (c + 1) * chunk)
            ckv = (_rms(ckv_ref[rows, :].astype(F32)) * kvlg_ref[...]).astype(BF16)
            kv = _dot(ckv, wukv_ref[...])
            kr = kr_ref[rows, :].astype(F32)
            kpe = _rope128(kr[:, :128], kr[:, 128:], kga_ref[...], kgb_ref[...],
                           cos_ref[rows, :], sin_ref[rows, :]).astype(BF16)
            for h in range(N_HEADS):
                sl = slice(h * HEAD_DIM, (h + 1) * HEAD_DIM)
                kf_ref[h, rows, 0:128] = (_rms(kv[:, sl]) * kgn_ref[...]).astype(BF16)
                kf_ref[h, rows, 128:256] = kpe
                _store_vt(vt_ref, h, c * (chunk // TK),
                          kv[:, GROUP_WIDTH + h * HEAD_DIM:GROUP_WIDTH + (h + 1) * HEAD_DIM])

    cq = (_rms(cq_ref[...].astype(F32)) * qlg_ref[...]).astype(BF16)
    qraw = _dot(cq, wuq_ref[...])
    qf = []
    for h in range(N_HEADS):
        base = h * MLA_Q_HEAD_COLS
        qn = _rms(qraw[:, base:base + 128]) * qgn_ref[...]
        qpe = _rope128(qraw[:, base + 128:base + 256], qraw[:, base + 256:base + 384],
                       qga_ref[...], qgb_ref[...], cosq_ref[...], sinq_ref[...])
        qf.append((jnp.concatenate([qn, qpe], axis=-1) * scale).astype(BF16))

    def kv(h, j):
        off = pl.multiple_of(j * TK, TK)
        return kf_ref[h, pl.ds(off, TK), :], vt_ref[j, h * VT_ROWS:(h + 1) * VT_ROWS, :], None

    _group_out(_flash_heads(i, qf, kv, flash), gn_ref, o_ref)


def _mla(proj, cos, sin, qlg, wuq, kvlg, wukv, qgn, qga, qgb, kgn, kga, kgb, gn):
    full = _whole
    consts = (qlg, wuq, kvlg, wukv, qgn, qga, qgb, kgn, kga, kgb, gn)
    return pl.pallas_call(
        _mla_kernel,
        grid=(BATCH, NQ),
        in_specs=[pl.BlockSpec((TQ, MLA_Q_RANK), lambda b, i: (b * NQ + i, COL_MCQ // MLA_Q_RANK)),
                  pl.BlockSpec((SEQ, MLA_KV_RANK), lambda b, i: (b, COL_MCKV // MLA_KV_RANK)),
                  pl.BlockSpec((SEQ, 256), lambda b, i: (b, COL_KROPE // 256)),
                  full(cos), full(sin),
                  pl.BlockSpec((TQ, 128), lambda b, i: (i, 0)),
                  pl.BlockSpec((TQ, 128), lambda b, i: (i, 0)),
                  *[full(c) for c in consts]],
        out_specs=pl.BlockSpec((TQ, GROUP_WIDTH), lambda b, i: (b * NQ + i, 0)),
        out_shape=jax.ShapeDtypeStruct((TOKENS, GROUP_WIDTH), BF16),
        scratch_shapes=[pltpu.VMEM((N_HEADS, SEQ, MLA_K_COLS), BF16),
                        pltpu.VMEM((NKV, N_HEADS * VT_ROWS, TK), BF16)] + _flash_scratch(),
        compiler_params=_cparams(("arbitrary", "arbitrary")),
        name="mla",
    )(proj, proj, proj, cos, sin, cos, sin, *[_arr(c) for c in consts])


def _gelu(x):
    return 0.5 * x * (1.0 + jnp.tanh(0.7978845608028654 * (x + 0.044715 * (x * x * x))))


def _gmlp_kernel(u_ref, v_ref, lg_ref, lb_ref, ws_ref, bs_ref, gn_ref, o_ref, mix_ref):
    v = _gelu(v_ref[...].astype(F32))
    mu = jnp.mean(v, axis=-1, keepdims=True)
    vc = v - mu
    var = jnp.mean(vc * vc, axis=-1, keepdims=True)
    vn = (vc * lax.rsqrt(var + EPS) * lg_ref[...] + lb_ref[...]).astype(BF16)
    r = lax.broadcasted_iota(jnp.int32, (GMLP_CHUNK, GMLP_CHUNK), 0)
    c = lax.broadcasted_iota(jnp.int32, (GMLP_CHUNK, GMLP_CHUNK), 1)
    for g in range(GROUP_WIDTH // HEAD_DIM):
        w = jnp.where(c <= r, ws_ref[g], 0.0).astype(BF16)
        cols = slice(g * HEAD_DIM, (g + 1) * HEAD_DIM)
        for n in range(ROW_TILE // GMLP_CHUNK):
            rows = slice(n * GMLP_CHUNK, (n + 1) * GMLP_CHUNK)
            mix_ref[rows, cols] = _dot(w, vn[rows, cols]) + bs_ref[:, cols]
    o = _gelu(u_ref[...].astype(F32)) * mix_ref[...]
    o_ref[...] = (_rms(o) * gn_ref[...]).astype(o_ref.dtype)


def _gmlp(proj, lg, lb, ws, bs, gn):
    consts = (lg, lb, ws, bs, gn)
    return pl.pallas_call(
        _gmlp_kernel,
        grid=(TOKENS // ROW_TILE,),
        in_specs=[pl.BlockSpec((ROW_TILE, GROUP_WIDTH), lambda m: (m, COL_GU // GROUP_WIDTH)),
                  pl.BlockSpec((ROW_TILE, GROUP_WIDTH), lambda m: (m, COL_GV // GROUP_WIDTH)),
                  *[_whole(c) for c in consts]],
        out_specs=pl.BlockSpec((ROW_TILE, GROUP_WIDTH), lambda m: (m, 0)),
        out_shape=jax.ShapeDtypeStruct((TOKENS, GROUP_WIDTH), BF16),
        scratch_shapes=[pltpu.VMEM((ROW_TILE, GROUP_WIDTH), F32)],
        compiler_params=_cparams(("arbitrary",)),
        name="gmlp",
    )(proj, proj, *[_arr(c) for c in consts])


def _moba_kernel(q_ref, k_ref, v_ref, qg_ref, kg_ref, gn_ref, o_ref, kn_ref, vt_ref, km_ref,
                 sel_ref, *flash):
    i = pl.program_id(1)
    scale = HEAD_DIM ** -0.5 * LOG2E

    @pl.when(i == 0)
    def _prep():
        for h in range(N_HEADS):
            sl = slice(h * HEAD_DIM, (h + 1) * HEAD_DIM)
            means = []
            for n in range(N_MOBA_BLOCKS):
                rows = slice(n * MOBA_BLOCK, (n + 1) * MOBA_BLOCK)
                kn = _rms(k_ref[rows, sl].astype(F32)) * kg_ref[...]
                kn_ref[rows, sl] = kn.astype(BF16)
                means.append(jnp.mean(kn, axis=0, keepdims=True))
            km_ref[h] = jnp.concatenate(means, axis=0)
            _store_vt(vt_ref, h, 0, v_ref[:, sl].astype(F32))

    blk = lax.broadcasted_iota(jnp.int32, (N_MOBA_BLOCKS, TQ), 0)
    sls = [slice(h * HEAD_DIM, (h + 1) * HEAD_DIM) for h in range(N_HEADS)]
    qn = []
    for h in range(N_HEADS):
        q32 = _rms(q_ref[:, sls[h]].astype(F32)) * qg_ref[...]
        qn.append((q32 * scale).astype(BF16))

        gate = _dot_nt(km_ref[h], q32, precision=lax.Precision.HIGHEST)
        gate = jnp.where(blk < i, gate, NEG)
        sel_t = jnp.zeros((N_MOBA_BLOCKS, TQ), F32)
        for n in range(N_MOBA_BLOCKS):
            gn_row = gate[n:n + 1, :]
            beats = jnp.where(gate > gn_row, 1.0, jnp.where((gate == gn_row) & (blk < n), 1.0, 0.0))
            rank = jnp.sum(beats, axis=0, keepdims=True)
            past = jnp.full((1, TQ), n, jnp.int32) < i
            chosen = jnp.where(rank < MOBA_TOPK, jnp.where(past, 1.0, 0.0), 0.0)
            sel_t = jnp.where(blk == n, chosen, sel_t)
        sel_ref[h] = sel_t

    def kv(h, j):
        off = pl.multiple_of(j * TK, TK)
        return kn_ref[pl.ds(off, TK), sls[h]], vt_ref[j, h * VT_ROWS:(h + 1) * VT_ROWS, :], None

    def keep(h, j):
        return sel_ref[h, pl.ds(j, 1), :]

    _group_out(_flash_heads(i, qn, kv, flash, keep), gn_ref, o_ref)


def _moba(proj, qg, kg, gn):
    return pl.pallas_call(
        _moba_kernel,
        grid=(BATCH, NQ),
        in_specs=[pl.BlockSpec((TQ, GROUP_WIDTH), lambda b, i: (b * NQ + i, COL_BQ // GROUP_WIDTH)),
                  pl.BlockSpec((SEQ, GROUP_WIDTH), lambda b, i: (b, COL_BK // GROUP_WIDTH)),
                  pl.BlockSpec((SEQ, GROUP_WIDTH), lambda b, i: (b, COL_BV // GROUP_WIDTH)),
                  _whole(qg), _whole(kg), _whole(gn)],
        out_specs=pl.BlockSpec((TQ, GROUP_WIDTH), lambda b, i: (b * NQ + i, 0)),
        out_shape=jax.ShapeDtypeStruct((TOKENS, GROUP_WIDTH), BF16),
        scratch_shapes=[pltpu.VMEM((SEQ, GROUP_WIDTH), BF16),
                        pltpu.VMEM((NKV, N_HEADS * VT_ROWS, TK), BF16),
                        pltpu.VMEM((N_HEADS, N_MOBA_BLOCKS, HEAD_DIM), F32),
                        pltpu.VMEM((N_HEADS, N_MOBA_BLOCKS, TQ), F32)] + _flash_scratch(),
        compiler_params=_cparams(("arbitrary", "arbitrary")),
        name="moba",
    )(proj, proj, proj, _arr(qg), _arr(kg), _arr(gn))


def _outproj_kernel(x_ref, a_ref, b_ref, c_ref, d_ref, w_ref, o_ref):
    mixed = jnp.concatenate([a_ref[...], b_ref[...], c_ref[...], d_ref[...]], axis=-1)
    o_ref[...] = x_ref[...] + _dot(mixed, w_ref[...])


def _outproj(x, oa, ob, oc, od, w):
    grp = pl.BlockSpec((ROW_TILE, GROUP_WIDTH), lambda m: (m, 0))
    return pl.pallas_call(
        _outproj_kernel,
        grid=(TOKENS // ROW_TILE,),
        in_specs=[pl.BlockSpec((ROW_TILE, D_MODEL), lambda m: (m, 0)), grp, grp, grp, grp, _whole(w)],
        out_specs=pl.BlockSpec((ROW_TILE, D_MODEL), lambda m: (m, 0)),
        out_shape=jax.ShapeDtypeStruct((TOKENS, D_MODEL), F32),
        compiler_params=_cparams(("arbitrary",)),
        name="outproj",
    )(x, oa, ob, oc, od, _arr(w))


def _memkv_kernel(m_ref, g_ref, w_ref, kg_ref, k_ref, v_ref):
    h = (_rms(m_ref[...]) * g_ref[...]).astype(BF16)
    kv = _dot(h, w_ref[...])
    for hd in range(N_HEADS):
        sl = slice(hd * HEAD_DIM, (hd + 1) * HEAD_DIM)
        k_ref[:, sl] = (_rms(kv[:, sl]) * kg_ref[...]).astype(BF16)
    v_ref[...] = kv[:, GROUP_WIDTH:].astype(BF16)


def _memkv(mem, g, w, kg):
    out = pl.BlockSpec((MEM_LEN, GROUP_WIDTH), lambda b: (b, 0))
    return pl.pallas_call(
        _memkv_kernel,
        grid=(BATCH,),
        in_specs=[pl.BlockSpec((MEM_LEN, D_MODEL), lambda b: (b, 0)),
                  _whole(g), _whole(w), _whole(kg)],
        out_specs=[out, out],
        out_shape=[jax.ShapeDtypeStruct((BATCH * MEM_LEN, GROUP_WIDTH), BF16)] * 2,
        compiler_params=_cparams(("arbitrary",)),
        name="memkv",
    )(mem, _arr(g), _arr(w), _arr(kg))


def _xattn_kernel(x_ref, g_ref, wq_ref, qg_ref, k_ref, v_ref, wo_ref, o_ref):
    x = x_ref[...]
    h = (_rms(x) * g_ref[...]).astype(BF16)
    q = _dot(h, wq_ref[...])
    scale = HEAD_DIM ** -0.5 * LOG2E
    heads = []
    for hd in range(N_HEADS):
        sl = slice(hd * HEAD_DIM, (hd + 1) * HEAD_DIM)
        qn = (_rms(q[:, sl]) * qg_ref[...] * scale).astype(BF16)
        s = _dot_nt(qn, k_ref[:, sl])
        p = jnp.exp2(s - jnp.max(s, axis=-1, keepdims=True))
        o = _dot(p.astype(BF16), v_ref[:, sl])
        heads.append(o / jnp.sum(p, axis=-1, keepdims=True))
    a = jnp.concatenate(heads, axis=-1).astype(BF16)
    o_ref[...] = x + _dot(a, wo_ref[...])


def _xattn(x, g, wq, qg, k, v, wo):
    per_seq = SEQ // ROW_TILE
    return pl.pallas_call(
        _xattn_kernel,
        grid=(TOKENS // ROW_TILE,),
        in_specs=[pl.BlockSpec((ROW_TILE, D_MODEL), lambda m: (m, 0)),
                  _whole(g), _whole(wq), _whole(qg),
                  pl.BlockSpec((MEM_LEN, GROUP_WIDTH), lambda m: (m // per_seq, 0)),
                  pl.BlockSpec((MEM_LEN, GROUP_WIDTH), lambda m: (m // per_seq, 0)),
                  _whole(wo)],
        out_specs=pl.BlockSpec((ROW_TILE, D_MODEL), lambda m: (m, 0)),
        out_shape=jax.ShapeDtypeStruct((TOKENS, D_MODEL), F32),
        compiler_params=_cparams(("arbitrary",)),
        name="xattn",
    )(x, _arr(g), _arr(wq), _arr(qg), k, v, _arr(wo))


def _ffn_kernel(x_ref, g_ref, wg_ref, wu_ref, wd_ref, o_ref, h_ref):
    @pl.when(pl.program_id(1) == 0)
    def _():
        x = x_ref[...]
        h_ref[...] = (_rms(x) * g_ref[...]).astype(BF16)
        o_ref[...] = x

    h = h_ref[...]
    gate = _dot(h, wg_ref[...].astype(BF16))
    up = _dot(h, wu_ref[...].astype(BF16))
    a = (gate * (1.0 / (1.0 + jnp.exp(-gate))) * up).astype(BF16)
    o_ref[...] += _dot(a, wd_ref[...].astype(BF16))


def _ffn(x, g, wgu, wd, l):
    bm, bf = FFN_BM, FFN_BF
    nf = D_FF // bf
    return pl.pallas_call(
        _ffn_kernel,
        grid=(TOKENS // bm, nf),
        in_specs=[pl.BlockSpec((bm, D_MODEL), lambda m, f: (m, 0)),
                  _whole(g),
                  pl.BlockSpec((None, D_MODEL, bf), lambda m, f: (l, 0, f)),
                  pl.BlockSpec((None, D_MODEL, bf), lambda m, f: (l, 0, nf + f)),
                  pl.BlockSpec((None, bf, D_MODEL), lambda m, f: (l, f, 0))],
        out_specs=pl.BlockSpec((bm, D_MODEL), lambda m, f: (m, 0)),
        out_shape=jax.ShapeDtypeStruct((TOKENS, D_MODEL), F32),
        scratch_shapes=[pltpu.VMEM((bm, D_MODEL), BF16)],
        compiler_params=_cparams(("arbitrary", "arbitrary")),
        name="ffn",
    )(x, _arr(g), wgu, wgu, wd)


def _swap_halves(g):
    half = MLA_ROPE_DIM // 2
    return jnp.concatenate([g[..., half:], g[..., :half]], axis=-1)


def _rope_tables():
    pos = jnp.arange(SEQ, dtype=F32)
    inv_freq = ROPE_THETA ** (-jnp.arange(0, MLA_ROPE_DIM, 2, dtype=F32) / MLA_ROPE_DIM)
    ang = pos[:, None] * inv_freq[None, :]
    zeros = jnp.zeros((SEQ, MLA_ROPE_DIM), F32)
    cos = jnp.concatenate([jnp.cos(ang), jnp.cos(ang), zeros], axis=-1)
    sin = jnp.concatenate([jnp.sin(ang), jnp.sin(ang), zeros], axis=-1)
    return cos, sin


def _rows(v):
    return v.reshape(-1, 1, v.shape[-1]).astype(F32)


def _stacked_params(p):
    wuq = p['mla_w_uq'].astype(BF16).reshape(DEPTH, MLA_Q_RANK, N_HEADS, MLA_QK_DIM)
    nope, pe = wuq[..., :MLA_NOPE_DIM], wuq[..., MLA_NOPE_DIM:]
    rot = _rot_cols(pe)
    wuq = jnp.concatenate([nope, pe, pe, rot, rot], axis=-1).reshape(
        DEPTH, MLA_Q_RANK, N_HEADS * MLA_Q_HEAD_COLS)
    wukv = p['mla_w_ukv'].astype(BF16).reshape(DEPTH, MLA_KV_RANK, N_HEADS, 2 * HEAD_DIM)
    wukv = jnp.concatenate([wukv[..., :HEAD_DIM].reshape(DEPTH, MLA_KV_RANK, -1),
                            wukv[..., HEAD_DIM:].reshape(DEPTH, MLA_KV_RANK, -1)], axis=-1)
    qg, kg = p['mla_q_norm'], p['mla_k_norm']
    two = lambda g: _rows(jnp.concatenate([g, g], axis=-1))
    return dict(
        mix_norm=_rows(p['mix_norm']),
        fox_b_f=_rows(jnp.pad(p['fox_b_f'], ((0, 0), (0, SMALL_COLS - N_HEADS)))),
        fox_q=_rows(p['fox_q_norm']), fox_k=_rows(p['fox_k_norm']),
        mla_qlg=_rows(p['mla_q_lora_norm']), mla_wuq=wuq,
        mla_kvlg=_rows(p['mla_kv_lora_norm']), mla_wukv=wukv,
        mla_qgn=_rows(qg[:, :MLA_NOPE_DIM]), mla_qga=two(qg[:, MLA_NOPE_DIM:]),
        mla_qgb=two(_swap_halves(qg[:, MLA_NOPE_DIM:])),
        mla_kgn=_rows(kg[:, :MLA_NOPE_DIM]), mla_kga=two(kg[:, MLA_NOPE_DIM:]),
        mla_kgb=two(_swap_halves(kg[:, MLA_NOPE_DIM:])),
        gmlp_lg=_rows(p['gmlp_ln_g']), gmlp_lb=_rows(p['gmlp_ln_b']),
        gmlp_ws=p['gmlp_w_s'].astype(F32),
        gmlp_bs=jnp.repeat(jnp.swapaxes(p['gmlp_b_s'], 1, 2), HEAD_DIM, axis=2).astype(F32),
        moba_q=_rows(p['moba_q_norm']), moba_k=_rows(p['moba_k_norm']),
        gn=_rows(p['group_norm']),
        w_out=p['w_out'].astype(BF16),
        xattn_norm=_rows(p['xattn_norm']), mem_norm=_rows(p['mem_norm']),
        w_xq=p['w_xq'].astype(BF16), w_xkv=p['w_xkv'].astype(BF16),
        xq=_rows(p['xattn_q_norm']), xk=_rows(p['xattn_k_norm']),
        w_xo=p['w_xo'].astype(BF16),
        ffn_norm=_rows(p['ffn_norm']),
    )


def kernel(x, mem, mix_norm, w_in, fox_b_f, fox_q_norm, fox_k_norm, mla_q_lora_norm, mla_w_uq,
           mla_kv_lora_norm, mla_w_ukv, mla_q_norm, mla_k_norm, gmlp_ln_g, gmlp_ln_b, gmlp_w_s,
           gmlp_b_s, moba_q_norm, moba_k_norm, group_norm, w_out, xattn_norm, mem_norm, w_xq, w_xkv,
           xattn_q_norm, xattn_k_norm, w_xo, ffn_norm, w_gate_up, w_down):
    params = dict(mix_norm=mix_norm, w_in=w_in, fox_b_f=fox_b_f, fox_q_norm=fox_q_norm,
                  fox_k_norm=fox_k_norm, mla_q_lora_norm=mla_q_lora_norm, mla_w_uq=mla_w_uq,
                  mla_kv_lora_norm=mla_kv_lora_norm, mla_w_ukv=mla_w_ukv, mla_q_norm=mla_q_norm,
                  mla_k_norm=mla_k_norm, gmlp_ln_g=gmlp_ln_g, gmlp_ln_b=gmlp_ln_b,
                  gmlp_w_s=gmlp_w_s, gmlp_b_s=gmlp_b_s, moba_q_norm=moba_q_norm,
                  moba_k_norm=moba_k_norm, group_norm=group_norm, w_out=w_out,
                  xattn_norm=xattn_norm, mem_norm=mem_norm, w_xq=w_xq, w_xkv=w_xkv,
                  xattn_q_norm=xattn_q_norm, xattn_k_norm=xattn_k_norm, w_xo=w_xo,
                  ffn_norm=ffn_norm, w_gate_up=w_gate_up, w_down=w_down)
    st = _stacked_params(params)
    cos, sin = _rope_tables()
    xf = x.reshape(TOKENS, D_MODEL)
    memf = mem.reshape(BATCH * MEM_LEN, D_MODEL)
    for l in range(DEPTH):
        q = {name: _Layer(a, l) for name, a in st.items()}
        gn = [_Layer(st['gn'], l * 4 + g) for g in range(4)]
        w_main, w_small = _relayout_w_in(w_in, l)
        proj, small = _inproj(xf, q['mix_norm'], w_main, w_small)
        o_a = _fox(proj, small, q['fox_b_f'], q['fox_q'], q['fox_k'], gn[0])
        o_b = _mla(proj, cos, sin, q['mla_qlg'], q['mla_wuq'], q['mla_kvlg'], q['mla_wukv'],
                   q['mla_qgn'], q['mla_qga'], q['mla_qgb'], q['mla_kgn'], q['mla_kga'],
                   q['mla_kgb'], gn[1])
        o_c = _gmlp(proj, q['gmlp_lg'], q['gmlp_lb'], q['gmlp_ws'], q['gmlp_bs'], gn[2])
        o_d = _moba(proj, q['moba_q'], q['moba_k'], gn[3])
        xf = _outproj(xf, o_a, o_b, o_c, o_d, q['w_out'])
        mk, mv = _memkv(memf, q['mem_norm'], q['w_xkv'], q['xk'])
        xf = _xattn(xf, q['xattn_norm'], q['w_xq'], q['xq'], mk, mv, q['w_xo'])
        xf = _ffn(xf, q['ffn_norm'], w_gate_up, w_down, l)
    return xf.reshape(BATCH, SEQ, D_MODEL)
```

```python
from typing import NamedTuple

import jax
import jax.numpy as jnp
from jax import lax
from jax.experimental import pallas as pl
from jax.experimental.pallas import tpu as pltpu

F32 = jnp.float32
BF16 = jnp.bfloat16

D_MODEL = 2048
BATCH = 8
SEQ = 2048
DEPTH = 2
TOKENS = BATCH * SEQ
MEM_LEN = 256
GROUP_WIDTH = 512
HEAD_DIM = 128
N_HEADS = 4
MLA_Q_RANK = 512
MLA_KV_RANK = 256
MLA_NOPE_DIM = 128
MLA_ROPE_DIM = 64
MLA_QK_DIM = MLA_NOPE_DIM + MLA_ROPE_DIM
ROPE_THETA = 10000.0
GMLP_CHUNK = 128
MOBA_BLOCK = 256
MOBA_TOPK = 3
N_MOBA_BLOCKS = SEQ // MOBA_BLOCK
D_FF = 5632
EPS = 1e-6
NEG = -1e30
M_FLOOR = -1e29
LOG2E = 1.4426950408889634

V7X_VMEM_LIMIT_BYTES = 56 * 1024 * 1024

COL_FQ, COL_FK, COL_FV = 0, 512, 1024
COL_MCQ = 1536
COL_GU, COL_GV = 2048, 2560
COL_BQ, COL_BK, COL_BV = 3072, 3584, 4096
COL_MCKV = 4608
COL_KROPE = 4864
PROJ_COLS = 5120
SMALL_COLS = 128

TQ = 512
TK = 256
NQ = SEQ // TQ
NKV = SEQ // TK
INPROJ_BM, INPROJ_BN = 1024, 1024
ROW_TILE = 512
FFN_BM, FFN_BF = 1024, 256


def _cparams(sem):
    return pltpu.CompilerParams(dimension_semantics=sem,
                                vmem_limit_bytes=V7X_VMEM_LIMIT_BYTES)


class _Layer(NamedTuple):
    stacked: jax.Array
    index: int


def _whole(p):
    if isinstance(p, _Layer):
        shape = p.stacked.shape
        return pl.BlockSpec((None,) + shape[1:], lambda *_: (p.index,) + (0,) * (len(shape) - 1))
    return pl.BlockSpec(p.shape, lambda *_: (0,) * p.ndim)


def _arr(p):
    return p.stacked if isinstance(p, _Layer) else p


def _rms(x):
    return x * lax.rsqrt(jnp.mean(x * x, axis=-1, keepdims=True) + EPS)


def _dot(a, b):
    return jnp.dot(a, b, preferred_element_type=F32)


def _dot_nt(a, b, precision=None):
    return lax.dot_general(a, b, (((1,), (1,)), ((), ())),
                           preferred_element_type=F32, precision=precision)


VT_ROWS = HEAD_DIM + 16


def _flash_scratch():
    return [pltpu.VMEM((2, N_HEADS, TK, TQ), F32), pltpu.VMEM((2, N_HEADS, 1, TQ), F32),
            pltpu.VMEM((N_HEADS, 1, TQ), F32), pltpu.VMEM((N_HEADS, VT_ROWS, TQ), F32)]


def _flash_heads(i, q, kv, scratch, keep=None):
    s_ref, cm_ref, m_ref, acc_ref = scratch
    assert TQ == 2 * TK
    key_idx = lax.broadcasted_iota(jnp.int32, (TK, TK), 0)
    qry_idx = lax.broadcasted_iota(jnp.int32, (TK, TK), 1)
    causal = key_idx <= qry_idx

    def mask_overlap(h, j, s, second):
        left, right = s[:, :TK], s[:, TK:]
        if second:
            left = jnp.full_like(left, NEG)
            right = jnp.where(causal, right, NEG)
        else:
            left = jnp.where(causal, left, NEG)
            if keep is not None:
                right = jnp.where(keep(h, j)[:, TK:] > 0.5, right, NEG)
        return jnp.concatenate([left, right], axis=1)

    def stage_a(j, slot, overlap=None):
        raw = [_dot_nt(kv(h, j)[0], q[h]) for h in range(N_HEADS)]
        for h in range(N_HEADS):
            s, bias = raw[h], kv(h, j)[2]
            if bias is not None:
                s = s - bias
            if overlap is not None:
                s = mask_overlap(h, j, s, overlap)
            elif keep is not None:
                s = jnp.where(keep(h, j) > 0.5, s, NEG)
            s_ref[slot, h] = s
            cm_ref[slot, h] = jnp.max(s, axis=0, keepdims=True)

    def stage_b(j, slot):
        probs, alphas = [], []
        for h in range(N_HEADS):
            m_old = m_ref[h]
            m_new = jnp.maximum(m_old, cm_ref[slot, h])
            m_ref[h] = m_new
            probs.append(jnp.exp2(s_ref[slot, h] - m_new).astype(BF16))
            alphas.append(jnp.exp2(m_old - m_new))
        for h in range(N_HEADS):
            acc_ref[h] = alphas[h] * acc_ref[h] + _dot(kv(h, j)[1], probs[h])

    m_ref[...] = jnp.full(m_ref.shape, M_FLOOR, F32)
    acc_ref[...] = jnp.zeros(acc_ref.shape, F32)
    stage_a(2 * i, 0, overlap=False)
    stage_a(2 * i + 1, 1, overlap=True)
    stage_b(2 * i, 0)

    def body(p, _):
        pending = jnp.where(p == 0, 2 * i + 1, 2 * p - 1)
        stage_a(2 * p, 0)
        stage_b(pending, 1)
        stage_a(2 * p + 1, 1)
        stage_b(2 * p, 0)
        return 0

    lax.fori_loop(0, i, body, 0)
    stage_b(jnp.where(i == 0, 1, 2 * i - 1), 1)
    return [(acc_ref[h, :HEAD_DIM] / acc_ref[h, HEAD_DIM:HEAD_DIM + 1]).T for h in range(N_HEADS)]


def _store_vt(vt_ref, h, first_tile, v32):
    base = h * VT_ROWS
    for t in range(v32.shape[0] // TK):
        vt_ref[first_tile + t, base:base + HEAD_DIM, :] = v32[t * TK:(t + 1) * TK].T.astype(BF16)
        vt_ref[first_tile + t, base + HEAD_DIM:base + VT_ROWS, :] = jnp.ones((VT_ROWS - HEAD_DIM, TK), BF16)


def _group_out(heads, gn_ref, o_ref):
    o = jnp.concatenate(heads, axis=-1)
    o_ref[...] = (_rms(o) * gn_ref[...]).astype(o_ref.dtype)


def _inproj_kernel(x_ref, g_ref, w_ref, ws_ref, o_ref, os_ref, h_ref):
    @pl.when(pl.program_id(1) == 0)
    def _():
        h = (_rms(x_ref[...]) * g_ref[...]).astype(BF16)
        h_ref[...] = h
        os_ref[...] = _dot_nt(h, ws_ref[...])

    o_ref[...] = _dot_nt(h_ref[...], w_ref[...]).astype(o_ref.dtype)


def _inproj(x, g, wt, wst, l):
    bm, bn = INPROJ_BM, INPROJ_BN
    return pl.pallas_call(
        _inproj_kernel,
        grid=(TOKENS // bm, PROJ_COLS // bn),
        in_specs=[pl.BlockSpec((bm, D_MODEL), lambda m, n: (m, 0)),
                  _whole(g),
                  pl.BlockSpec((None, bn, D_MODEL), lambda m, n: (l, n, 0)),
                  pl.BlockSpec((None, SMALL_COLS, D_MODEL), lambda m, n: (l, 0, 0))],
        out_specs=[pl.BlockSpec((bm, bn), lambda m, n: (m, n)),
                   pl.BlockSpec((bm, SMALL_COLS), lambda m, n: (m, 0))],
        out_shape=[jax.ShapeDtypeStruct((TOKENS, PROJ_COLS), BF16),
                   jax.ShapeDtypeStruct((TOKENS, SMALL_COLS), F32)],
        scratch_shapes=[pltpu.VMEM((bm, D_MODEL), BF16)],
        compiler_params=_cparams(("arbitrary", "arbitrary")),
        name="inproj",
    )(x, _arr(g), wt, wst)


IN_SPLITS = (512, 512, 512, 4, 512, 256, 64, 512, 512, 512, 512, 512)
IN_COLS = sum(IN_SPLITS)
RELAYOUT_COLS = 256


def _rot_cols(w):
    half = MLA_ROPE_DIM // 2
    return jnp.concatenate([-w[..., half:], w[..., :half]], axis=-1)


def _relayout_kernel(w_ref, o_ref, s_ref):
    off = [0]
    for n in IN_SPLITS:
        off.append(off[-1] + n)
    half = MLA_ROPE_DIM // 2
    for l in range(DEPTH):
        w = w_ref[:, l, :]
        fq, fk, fv, ff, mcq, mckv, mkr, gu, gv, bq, bk, bv = (w[off[k]:off[k + 1]] for k in range(12))
        rot = jnp.concatenate([-mkr[half:], mkr[:half]], axis=0)
        o_ref[l] = jnp.concatenate([fq, fk, fv, mcq, gu, gv, bq, bk, bv, mckv, mkr, mkr, rot, rot],
                                   axis=0).astype(BF16)
        pad = jnp.zeros((SMALL_COLS - N_HEADS, w.shape[1]), F32)
        s_ref[l] = jnp.concatenate([ff, pad], axis=0).astype(BF16)


def _relayout_w_in(w_in):
    cols = RELAYOUT_COLS
    return pl.pallas_call(
        _relayout_kernel,
        grid=(D_MODEL // cols,),
        in_specs=[pl.BlockSpec((IN_COLS, DEPTH, cols), lambda m: (0, 0, m))],
        out_specs=[pl.BlockSpec((DEPTH, PROJ_COLS, cols), lambda m: (0, 0, m)),
                   pl.BlockSpec((DEPTH, SMALL_COLS, cols), lambda m: (0, 0, m))],
        out_shape=[jax.ShapeDtypeStruct((DEPTH, PROJ_COLS, D_MODEL), BF16),
                   jax.ShapeDtypeStruct((DEPTH, SMALL_COLS, D_MODEL), BF16)],
        compiler_params=_cparams(("arbitrary",)),
        name="relayout_w_in",
    )(jnp.transpose(w_in, (2, 0, 1)))


def _fox_kernel(q_ref, k_ref, v_ref, f_ref, bf_ref, qg_ref, kg_ref, gn_ref, o_ref,
                kn_ref, vt_ref, c_ref, *flash):
    i = pl.program_id(1)
    scale = HEAD_DIM ** -0.5 * LOG2E
    sls = [slice(h * HEAD_DIM, (h + 1) * HEAD_DIM) for h in range(N_HEADS)]

    @pl.when(i == 0)
    def _prep():
        x = f_ref[...] + bf_ref[...]
        c = jnp.minimum(x, 0.0) - jnp.log1p(jnp.exp(-jnp.abs(x)))
        row = lax.broadcasted_iota(jnp.int32, c.shape, 0)
        step = 1
        while step < SEQ:
            c = c + jnp.where(row >= step, pltpu.roll(c, step, 0), 0.0)
            step *= 2
        for h in range(N_HEADS):
            kn_ref[:, sls[h]] = (_rms(k_ref[:, sls[h]].astype(F32)) * kg_ref[...]).astype(BF16)
            _store_vt(vt_ref, h, 0, v_ref[:, sls[h]].astype(F32))
            c_ref[h] = jnp.broadcast_to(c[:, h:h + 1] * LOG2E, (SEQ, HEAD_DIM))

    q = [(_rms(q_ref[:, sl].astype(F32)) * qg_ref[...] * scale).astype(BF16) for sl in sls]

    def kv(h, j):
        off = pl.multiple_of(j * TK, TK)
        c = c_ref[h, pl.ds(off, TK), :]
        return (kn_ref[pl.ds(off, TK), sls[h]],
                vt_ref[j, h * VT_ROWS:(h + 1) * VT_ROWS, :],
                jnp.concatenate([c] * (TQ // HEAD_DIM), axis=1))

    _group_out(_flash_heads(i, q, kv, flash), gn_ref, o_ref)


def _fox(proj, small, b_f, qg, kg, gn):
    return pl.pallas_call(
        _fox_kernel,
        grid=(BATCH, NQ),
        in_specs=[pl.BlockSpec((TQ, GROUP_WIDTH), lambda b, i: (b * NQ + i, COL_FQ // GROUP_WIDTH)),
                  pl.BlockSpec((SEQ, GROUP_WIDTH), lambda b, i: (b, COL_FK // GROUP_WIDTH)),
                  pl.BlockSpec((SEQ, GROUP_WIDTH), lambda b, i: (b, COL_FV // GROUP_WIDTH)),
                  pl.BlockSpec((SEQ, SMALL_COLS), lambda b, i: (b, 0)),
                  _whole(b_f), _whole(qg), _whole(kg), _whole(gn)],
        out_specs=pl.BlockSpec((TQ, GROUP_WIDTH), lambda b, i: (b * NQ + i, 0)),
        out_shape=jax.ShapeDtypeStruct((TOKENS, GROUP_WIDTH), BF16),
        scratch_shapes=[pltpu.VMEM((SEQ, GROUP_WIDTH), BF16),
                        pltpu.VMEM((NKV, N_HEADS * VT_ROWS, TK), BF16),
                        pltpu.VMEM((N_HEADS, SEQ, HEAD_DIM), F32)] + _flash_scratch(),
        compiler_params=_cparams(("arbitrary", "arbitrary")),
        name="fox",
    )(proj, proj, proj, small, _arr(b_f), _arr(qg), _arr(kg), _arr(gn))


MLA_Q_HEAD_COLS = 384
MLA_K_COLS = 256


def _rope128(a, b, ga, gb, cos, sin):
    r = lax.rsqrt(jnp.sum(a * a, axis=-1, keepdims=True) * (1.0 / (2 * MLA_ROPE_DIM)) + EPS)
    return r * (a * ga * cos + b * gb * sin)


def _mla_kernel(cq_ref, ckv_ref, kr_ref, cos_ref, sin_ref, cosq_ref, sinq_ref,
                qlg_ref, wuq_ref, kvlg_ref, wukv_ref, qgn_ref, qga_ref, qgb_ref,
                kgn_ref, kga_ref, kgb_ref, gn_ref, o_ref, kf_ref, vt_ref, *flash):
    i = pl.program_id(1)
    scale = MLA_QK_DIM ** -0.5 * LOG2E
    chunk = 512

    @pl.when(i == 0)
    def _prep():
        for c in range(SEQ // chunk):
            rows = slice(c * chunk, (c + 1) * chunk)
            ckv = (_rms(ckv_ref[rows, :].astype(F32)) * kvlg_ref[...]).astype(BF16)
            kv = _dot(ckv, wukv_ref[...])
            kr = kr_ref[rows, :].astype(F32)
            kpe = _rope128(kr[:, :128], kr[:, 128:], kga_ref[...], kgb_ref[...],
                           cos_ref[rows, :], sin_ref[rows, :]).astype(BF16)
            for h in range(N_HEADS):
                sl = slice(h * HEAD_DIM, (h + 1) * HEAD_DIM)
                kf_ref[h, rows, 0:128] = (_rms(kv[:, sl]) * kgn_ref[...]).astype(BF16)
                kf_ref[h, rows, 128:256] = kpe
                _store_vt(vt_ref, h, c * (chunk // TK),
                          kv[:, GROUP_WIDTH + h * HEAD_DIM:GROUP_WIDTH + (h + 1) * HEAD_DIM])

    cq = (_rms(cq_ref[...].astype(F32)) * qlg_ref[...]).astype(BF16)
    qraw = _dot(cq, wuq_ref[...])
    qf = []
    for h in range(N_HEADS):
        base = h * MLA_Q_HEAD_COLS
        qn = _rms(qraw[:, base:base + 128]) * qgn_ref[...]
        qpe = _rope128(qraw[:, base + 128:base + 256], qraw[:, base + 256:base + 384],
                       qga_ref[...], qgb_ref[...], cosq_ref[...], sinq_ref[...])
        qf.append((jnp.concatenate([qn, qpe], axis=-1) * scale).astype(BF16))

    def kv(h, j):
        off = pl.multiple_of(j * TK, TK)
        return kf_ref[h, pl.ds(off, TK), :], vt_ref[j, h * VT_ROWS:(h + 1) * VT_ROWS, :], None

    _group_out(_flash_heads(i, qf, kv, flash), gn_ref, o_ref)


def _mla(proj, cos, sin, qlg, wuq, kvlg, wukv, qgn, qga, qgb, kgn, kga, kgb, gn):
    full = _whole
    consts = (qlg, wuq, kvlg, wukv, qgn, qga, qgb, kgn, kga, kgb, gn)
    return pl.pallas_call(
        _mla_kernel,
        grid=(BATCH, NQ),
        in_specs=[pl.BlockSpec((TQ, MLA_Q_RANK), lambda b, i: (b * NQ + i, COL_MCQ // MLA_Q_RANK)),
                  pl.BlockSpec((SEQ, MLA_KV_RANK), lambda b, i: (b, COL_MCKV // MLA_KV_RANK)),
                  pl.BlockSpec((SEQ, 256), lambda b, i: (b, COL_KROPE // 256)),
                  full(cos), full(sin),
                  pl.BlockSpec((TQ, 128), lambda b, i: (i, 0)),
                  pl.BlockSpec((TQ, 128), lambda b, i: (i, 0)),
                  *[full(c) for c in consts]],
        out_specs=pl.BlockSpec((TQ, GROUP_WIDTH), lambda b, i: (b * NQ + i, 0)),
        out_shape=jax.ShapeDtypeStruct((TOKENS, GROUP_WIDTH), BF16),
        scratch_shapes=[pltpu.VMEM((N_HEADS, SEQ, MLA_K_COLS), BF16),
                        pltpu.VMEM((NKV, N_HEADS * VT_ROWS, TK), BF16)] + _flash_scratch(),
        compiler_params=_cparams(("arbitrary", "arbitrary")),
        name="mla",
    )(proj, proj, proj, cos, sin, cos, sin, *[_arr(c) for c in consts])


def _gelu(x):
    return 0.5 * x * (1.0 + jnp.tanh(0.7978845608028654 * (x + 0.044715 * (x * x * x))))


def _gmlp_kernel(u_ref, v_ref, lg_ref, lb_ref, ws_ref, bs_ref, gn_ref, o_ref, mix_ref):
    v = _gelu(v_ref[...].astype(F32))
    mu = jnp.mean(v, axis=-1, keepdims=True)
    vc = v - mu
    var = jnp.mean(vc * vc, axis=-1, keepdims=True)
    vn = (vc * lax.rsqrt(var + EPS) * lg_ref[...] + lb_ref[...]).astype(BF16)
    r = lax.broadcasted_iota(jnp.int32, (GMLP_CHUNK, GMLP_CHUNK), 0)
    c = lax.broadcasted_iota(jnp.int32, (GMLP_CHUNK, GMLP_CHUNK), 1)
    for g in range(GROUP_WIDTH // HEAD_DIM):
        w = jnp.where(c <= r, ws_ref[g], 0.0).astype(BF16)
        cols = slice(g * HEAD_DIM, (g + 1) * HEAD_DIM)
        for n in range(ROW_TILE // GMLP_CHUNK):
            rows = slice(n * GMLP_CHUNK, (n + 1) * GMLP_CHUNK)
            mix_ref[rows, cols] = _dot(w, vn[rows, cols]) + bs_ref[:, cols]
    o = _gelu(u_ref[...].astype(F32)) * mix_ref[...]
    o_ref[...] = (_rms(o) * gn_ref[...]).astype(o_ref.dtype)


def _gmlp(proj, lg, lb, ws, bs, gn):
    consts = (lg, lb, ws, bs, gn)
    return pl.pallas_call(
        _gmlp_kernel,
        grid=(TOKENS // ROW_TILE,),
        in_specs=[pl.BlockSpec((ROW_TILE, GROUP_WIDTH), lambda m: (m, COL_GU // GROUP_WIDTH)),
                  pl.BlockSpec((ROW_TILE, GROUP_WIDTH), lambda m: (m, COL_GV // GROUP_WIDTH)),
                  *[_whole(c) for c in consts]],
        out_specs=pl.BlockSpec((ROW_TILE, GROUP_WIDTH), lambda m: (m, 0)),
        out_shape=jax.ShapeDtypeStruct((TOKENS, GROUP_WIDTH), BF16),
        scratch_shapes=[pltpu.VMEM((ROW_TILE, GROUP_WIDTH), F32)],
        compiler_params=_cparams(("arbitrary",)),
        name="gmlp",
    )(proj, proj, *[_arr(c) for c in consts])


def _moba_kernel(q_ref, k_ref, v_ref, qg_ref, kg_ref, gn_ref, o_ref, kn_ref, vt_ref, km_ref,
                 sel_ref, *flash):
    i = pl.program_id(1)
    scale = HEAD_DIM ** -0.5 * LOG2E

    @pl.when(i == 0)
    def _prep():
        for h in range(N_HEADS):
            sl = slice(h * HEAD_DIM, (h + 1) * HEAD_DIM)
            means = []
            for n in range(N_MOBA_BLOCKS):
                rows = slice(n * MOBA_BLOCK, (n + 1) * MOBA_BLOCK)
                kn = _rms(k_ref[rows, sl].astype(F32)) * kg_ref[...]
                kn_ref[rows, sl] = kn.astype(BF16)
                means.append(jnp.mean(kn, axis=0, keepdims=True))
            km_ref[h] = jnp.concatenate(means, axis=0)
            _store_vt(vt_ref, h, 0, v_ref[:, sl].astype(F32))

    blk = lax.broadcasted_iota(jnp.int32, (N_MOBA_BLOCKS, TQ), 0)
    second = lax.broadcasted_iota(jnp.int32, (1, TQ), 1) >= MOBA_BLOCK
    own = 2 * i + jnp.where(second, 1, 0)
    sls = [slice(h * HEAD_DIM, (h + 1) * HEAD_DIM) for h in range(N_HEADS)]
    qn = []
    for h in range(N_HEADS):
        q32 = _rms(q_ref[:, sls[h]].astype(F32)) * qg_ref[...]
        qn.append((q32 * scale).astype(BF16))

        gate = _dot_nt(km_ref[h], q32, precision=lax.Precision.HIGHEST)
        gate = jnp.where(blk < own, gate, NEG)
        sel_t = jnp.zeros((N_MOBA_BLOCKS, TQ), F32)
        for n in range(N_MOBA_BLOCKS):
            gn_row = gate[n:n + 1, :]
            beats = jnp.where(gate > gn_row, 1.0, jnp.where((gate == gn_row) & (blk < n), 1.0, 0.0))
            rank = jnp.sum(beats, axis=0, keepdims=True)
            past = n < own
            chosen = jnp.where(rank < MOBA_TOPK, jnp.where(past, 1.0, 0.0), 0.0)
            sel_t = jnp.where(blk == n, chosen, sel_t)
        sel_ref[h] = sel_t

    def kv(h, j):
        off = pl.multiple_of(j * TK, TK)
        return kn_ref[pl.ds(off, TK), sls[h]], vt_ref[j, h * VT_ROWS:(h + 1) * VT_ROWS, :], None

    def keep(h, j):
        return sel_ref[h, pl.ds(j, 1), :]

    _group_out(_flash_heads(i, qn, kv, flash, keep), gn_ref, o_ref)


def _moba(proj, qg, kg, gn):
    return pl.pallas_call(
        _moba_kernel,
        grid=(BATCH, NQ),
        in_specs=[pl.BlockSpec((TQ, GROUP_WIDTH), lambda b, i: (b * NQ + i, COL_BQ // GROUP_WIDTH)),
                  pl.BlockSpec((SEQ, GROUP_WIDTH), lambda b, i: (b, COL_BK // GROUP_WIDTH)),
                  pl.BlockSpec((SEQ, GROUP_WIDTH), lambda b, i: (b, COL_BV // GROUP_WIDTH)),
                  _whole(qg), _whole(kg), _whole(gn)],
        out_specs=pl.BlockSpec((TQ, GROUP_WIDTH), lambda b, i: (b * NQ + i, 0)),
        out_shape=jax.ShapeDtypeStruct((TOKENS, GROUP_WIDTH), BF16),
        scratch_shapes=[pltpu.VMEM((SEQ, GROUP_WIDTH), BF16),
                        pltpu.VMEM((NKV, N_HEADS * VT_ROWS, TK), BF16),
                        pltpu.VMEM((N_HEADS, N_MOBA_BLOCKS, HEAD_DIM), F32),
                        pltpu.VMEM((N_HEADS, N_MOBA_BLOCKS, TQ), F32)] + _flash_scratch(),
        compiler_params=_cparams(("arbitrary", "arbitrary")),
        name="moba",
    )(proj, proj, proj, _arr(qg), _arr(kg), _arr(gn))


def _outproj_kernel(x_ref, a_ref, b_ref, c_ref, d_ref, w_ref, o_ref):
    mixed = jnp.concatenate([a_ref[...], b_ref[...], c_ref[...], d_ref[...]], axis=-1)
    o_ref[...] = x_ref[...] + _dot(mixed, w_ref[...])


def _outproj(x, oa, ob, oc, od, w):
    grp = pl.BlockSpec((ROW_TILE, GROUP_WIDTH), lambda m: (m, 0))
    return pl.pallas_call(
        _outproj_kernel,
        grid=(TOKENS // ROW_TILE,),
        in_specs=[pl.BlockSpec((ROW_TILE, D_MODEL), lambda m: (m, 0)), grp, grp, grp, grp, _whole(w)],
        out_specs=pl.BlockSpec((ROW_TILE, D_MODEL), lambda m: (m, 0)),
        out_shape=jax.ShapeDtypeStruct((TOKENS, D_MODEL), F32),
        compiler_params=_cparams(("arbitrary",)),
        name="outproj",
    )(x, oa, ob, oc, od, _arr(w))


def _memkv_kernel(m_ref, g_ref, w_ref, kg_ref, k_ref, v_ref):
    h = (_rms(m_ref[...]) * g_ref[...]).astype(BF16)
    kv = _dot(h, w_ref[...])
    for hd in range(N_HEADS):
        sl = slice(hd * HEAD_DIM, (hd + 1) * HEAD_DIM)
        k_ref[:, sl] = (_rms(kv[:, sl]) * kg_ref[...]).astype(BF16)
    v_ref[...] = kv[:, GROUP_WIDTH:].astype(BF16)


def _memkv(mem, g, w, kg):
    out = pl.BlockSpec((MEM_LEN, GROUP_WIDTH), lambda b: (b, 0))
    return pl.pallas_call(
        _memkv_kernel,
        grid=(BATCH,),
        in_specs=[pl.BlockSpec((MEM_LEN, D_MODEL), lambda b: (b, 0)),
                  _whole(g), _whole(w), _whole(kg)],
        out_specs=[out, out],
        out_shape=[jax.ShapeDtypeStruct((BATCH * MEM_LEN, GROUP_WIDTH), BF16)] * 2,
        compiler_params=_cparams(("arbitrary",)),
        name="memkv",
    )(mem, _arr(g), _arr(w), _arr(kg))


def _xattn_kernel(x_ref, g_ref, wq_ref, qg_ref, k_ref, v_ref, wo_ref, o_ref):
    x = x_ref[...]
    h = (_rms(x) * g_ref[...]).astype(BF16)
    q = _dot(h, wq_ref[...])
    scale = HEAD_DIM ** -0.5 * LOG2E
    heads = []
    for hd in range(N_HEADS):
        sl = slice(hd * HEAD_DIM, (hd + 1) * HEAD_DIM)
        qn = (_rms(q[:, sl]) * qg_ref[...] * scale).astype(BF16)
        s = _dot_nt(qn, k_ref[:, sl])
        p = jnp.exp2(s - jnp.max(s, axis=-1, keepdims=True))
        o = _dot(p.astype(BF16), v_ref[:, sl])
        heads.append(o / jnp.sum(p, axis=-1, keepdims=True))
    a = jnp.concatenate(heads, axis=-1).astype(BF16)
    o_ref[...] = x + _dot(a, wo_ref[...])


def _xattn(x, g, wq, qg, k, v, wo):
    per_seq = SEQ // ROW_TILE
    return pl.pallas_call(
        _xattn_kernel,
        grid=(TOKENS // ROW_TILE,),
        in_specs=[pl.BlockSpec((ROW_TILE, D_MODEL), lambda m: (m, 0)),
                  _whole(g), _whole(wq), _whole(qg),
                  pl.BlockSpec((MEM_LEN, GROUP_WIDTH), lambda m: (m // per_seq, 0)),
                  pl.BlockSpec((MEM_LEN, GROUP_WIDTH), lambda m: (m // per_seq, 0)),
                  _whole(wo)],
        out_specs=pl.BlockSpec((ROW_TILE, D_MODEL), lambda m: (m, 0)),
        out_shape=jax.ShapeDtypeStruct((TOKENS, D_MODEL), F32),
        compiler_params=_cparams(("arbitrary",)),
        name="xattn",
    )(x, _arr(g), _arr(wq), _arr(qg), k, v, _arr(wo))


def _ffn_kernel(x_ref, g_ref, wg_ref, wu_ref, wd_ref, o_ref, h_ref):
    @pl.when(pl.program_id(1) == 0)
    def _():
        x = x_ref[...]
        h_ref[...] = (_rms(x) * g_ref[...]).astype(BF16)
        o_ref[...] = x

    h = h_ref[...]
    gate = _dot(h, wg_ref[...].astype(BF16))
    up = _dot(h, wu_ref[...].astype(BF16))
    a = (gate * (1.0 / (1.0 + jnp.exp(-gate))) * up).astype(BF16)
    o_ref[...] += _dot(a, wd_ref[...].astype(BF16))


def _ffn(x, g, wgu, wd, l):
    bm, bf = FFN_BM, FFN_BF
    nf = D_FF // bf
    return pl.pallas_call(
        _ffn_kernel,
        grid=(TOKENS // bm, nf),
        in_specs=[pl.BlockSpec((bm, D_MODEL), lambda m, f: (m, 0)),
                  _whole(g),
                  pl.BlockSpec((None, D_MODEL, bf), lambda m, f: (l, 0, f)),
                  pl.BlockSpec((None, D_MODEL, bf), lambda m, f: (l, 0, nf + f)),
                  pl.BlockSpec((None, bf, D_MODEL), lambda m, f: (l, f, 0))],
        out_specs=pl.BlockSpec((bm, D_MODEL), lambda m, f: (m, 0)),
        out_shape=jax.ShapeDtypeStruct((TOKENS, D_MODEL), F32),
        scratch_shapes=[pltpu.VMEM((bm, D_MODEL), BF16)],
        compiler_params=_cparams(("arbitrary", "arbitrary")),
        name="ffn",
    )(x, _arr(g), wgu, wgu, wd)


def _swap_halves(g):
    half = MLA_ROPE_DIM // 2
    return jnp.concatenate([g[..., half:], g[..., :half]], axis=-1)


def _rope_tables():
    pos = jnp.arange(SEQ, dtype=F32)
    inv_freq = ROPE_THETA ** (-jnp.arange(0, MLA_ROPE_DIM, 2, dtype=F32) / MLA_ROPE_DIM)
    ang = pos[:, None] * inv_freq[None, :]
    zeros = jnp.zeros((SEQ, MLA_ROPE_DIM), F32)
    cos = jnp.concatenate([jnp.cos(ang), jnp.cos(ang), zeros], axis=-1)
    sin = jnp.concatenate([jnp.sin(ang), jnp.sin(ang), zeros], axis=-1)
    return cos, sin


def _rows(v):
    return v.reshape(-1, 1, v.shape[-1]).astype(F32)


def _stacked_params(p):
    wuq = p['mla_w_uq'].astype(BF16).reshape(DEPTH, MLA_Q_RANK, N_HEADS, MLA_QK_DIM)
    nope, pe = wuq[..., :MLA_NOPE_DIM], wuq[..., MLA_NOPE_DIM:]
    rot = _rot_cols(pe)
    wuq = jnp.concatenate([nope, pe, pe, rot, rot], axis=-1).reshape(
        DEPTH, MLA_Q_RANK, N_HEADS * MLA_Q_HEAD_COLS)
    wukv = p['mla_w_ukv'].astype(BF16).reshape(DEPTH, MLA_KV_RANK, N_HEADS, 2 * HEAD_DIM)
    wukv = jnp.concatenate([wukv[..., :HEAD_DIM].reshape(DEPTH, MLA_KV_RANK, -1),
                            wukv[..., HEAD_DIM:].reshape(DEPTH, MLA_KV_RANK, -1)], axis=-1)
    qg, kg = p['mla_q_norm'], p['mla_k_norm']
    two = lambda g: _rows(jnp.concatenate([g, g], axis=-1))
    return dict(
        mix_norm=_rows(p['mix_norm']),
        fox_b_f=_rows(jnp.pad(p['fox_b_f'], ((0, 0), (0, SMALL_COLS - N_HEADS)))),
        fox_q=_rows(p['fox_q_norm']), fox_k=_rows(p['fox_k_norm']),
        mla_qlg=_rows(p['mla_q_lora_norm']), mla_wuq=wuq,
        mla_kvlg=_rows(p['mla_kv_lora_norm']), mla_wukv=wukv,
        mla_qgn=_rows(qg[:, :MLA_NOPE_DIM]), mla_qga=two(qg[:, MLA_NOPE_DIM:]),
        mla_qgb=two(_swap_halves(qg[:, MLA_NOPE_DIM:])),
        mla_kgn=_rows(kg[:, :MLA_NOPE_DIM]), mla_kga=two(kg[:, MLA_NOPE_DIM:]),
        mla_kgb=two(_swap_halves(kg[:, MLA_NOPE_DIM:])),
        gmlp_lg=_rows(p['gmlp_ln_g']), gmlp_lb=_rows(p['gmlp_ln_b']),
        gmlp_ws=p['gmlp_w_s'].astype(F32),
        gmlp_bs=jnp.repeat(jnp.swapaxes(p['gmlp_b_s'], 1, 2), HEAD_DIM, axis=2).astype(F32),
        moba_q=_rows(p['moba_q_norm']), moba_k=_rows(p['moba_k_norm']),
        gn=_rows(p['group_norm']),
        w_out=p['w_out'].astype(BF16),
        xattn_norm=_rows(p['xattn_norm']), mem_norm=_rows(p['mem_norm']),
        w_xq=p['w_xq'].astype(BF16), w_xkv=p['w_xkv'].astype(BF16),
        xq=_rows(p['xattn_q_norm']), xk=_rows(p['xattn_k_norm']),
        w_xo=p['w_xo'].astype(BF16),
        ffn_norm=_rows(p['ffn_norm']),
    )


def kernel(x, mem, mix_norm, w_in, fox_b_f, fox_q_norm, fox_k_norm, mla_q_lora_norm, mla_w_uq,
           mla_kv_lora_norm, mla_w_ukv, mla_q_norm, mla_k_norm, gmlp_ln_g, gmlp_ln_b, gmlp_w_s,
           gmlp_b_s, moba_q_norm, moba_k_norm, group_norm, w_out, xattn_norm, mem_norm, w_xq, w_xkv,
           xattn_q_norm, xattn_k_norm, w_xo, ffn_norm, w_gate_up, w_down):
    params = dict(mix_norm=mix_norm, w_in=w_in, fox_b_f=fox_b_f, fox_q_norm=fox_q_norm,
                  fox_k_norm=fox_k_norm, mla_q_lora_norm=mla_q_lora_norm, mla_w_uq=mla_w_uq,
                  mla_kv_lora_norm=mla_kv_lora_norm, mla_w_ukv=mla_w_ukv, mla_q_norm=mla_q_norm,
                  mla_k_norm=mla_k_norm, gmlp_ln_g=gmlp_ln_g, gmlp_ln_b=gmlp_ln_b,
                  gmlp_w_s=gmlp_w_s, gmlp_b_s=gmlp_b_s, moba_q_norm=moba_q_norm,
                  moba_k_norm=moba_k_norm, group_norm=group_norm, w_out=w_out,
                  xattn_norm=xattn_norm, mem_norm=mem_norm, w_xq=w_xq, w_xkv=w_xkv,
                  xattn_q_norm=xattn_q_norm, xattn_k_norm=xattn_k_norm, w_xo=w_xo,
                  ffn_norm=ffn_norm, w_gate_up=w_gate_up, w_down=w_down)
    st = _stacked_params(params)
    w_main_t, w_small_t = _relayout_w_in(w_in)
    cos, sin = _rope_tables()
    xf = x.reshape(TOKENS, D_MODEL)
    memf = mem.reshape(BATCH * MEM_LEN, D_MODEL)
    for l in range(DEPTH):
        q = {name: _Layer(a, l) for name, a in st.items()}
        gn = [_Layer(st['gn'], l * 4 + g) for g in range(4)]
        proj, small = _inproj(xf, q['mix_norm'], w_main_t, w_small_t, l)
        o_a = _fox(proj, small, q['fox_b_f'], q['fox_q'], q['fox_k'], gn[0])
        o_b = _mla(proj, cos, sin, q['mla_qlg'], q['mla_wuq'], q['mla_kvlg'], q['mla_wukv'],
                   q['mla_qgn'], q['mla_qga'], q['mla_qgb'], q['mla_kgn'], q['mla_kga'],
                   q['mla_kgb'], gn[1])
        o_c = _gmlp(proj, q['gmlp_lg'], q['gmlp_lb'], q['gmlp_ws'], q['gmlp_bs'], gn[2])
        o_d = _moba(proj, q['moba_q'], q['moba_k'], gn[3])
        xf = _outproj(xf, o_a, o_b, o_c, o_d, q['w_out'])
        mk, mv = _memkv(memf, q['mem_norm'], q['w_xkv'], q['xk'])
        xf = _xattn(xf, q['xattn_norm'], q['w_xq'], q['xq'], mk, mv, q['w_xo'])
        xf = _ffn(xf, q['ffn_norm'], w_gate_up, w_down, l)
    return xf.reshape(BATCH, SEQ, D_MODEL)
```

```python
from typing import NamedTuple

import jax
import jax.numpy as jnp
from jax import lax
from jax.experimental import pallas as pl
from jax.experimental.pallas import tpu as pltpu

F32 = jnp.float32
BF16 = jnp.bfloat16

D_MODEL = 2048
BATCH = 8
SEQ = 2048
DEPTH = 2
TOKENS = BATCH * SEQ
MEM_LEN = 256
GROUP_WIDTH = 512
HEAD_DIM = 128
N_HEADS = 4
MLA_Q_RANK = 512
MLA_KV_RANK = 256
MLA_NOPE_DIM = 128
MLA_ROPE_DIM = 64
MLA_QK_DIM = MLA_NOPE_DIM + MLA_ROPE_DIM
ROPE_THETA = 10000.0
GMLP_CHUNK = 128
MOBA_BLOCK = 256
MOBA_TOPK = 3
N_MOBA_BLOCKS = SEQ // MOBA_BLOCK
D_FF = 5632
EPS = 1e-6
NEG = -1e30
M_FLOOR = -1e29
LOG2E = 1.4426950408889634

V7X_VMEM_LIMIT_BYTES = 56 * 1024 * 1024

COL_FQ, COL_FK, COL_FV = 0, 512, 1024
COL_MCQ = 1536
COL_GU, COL_GV = 2048, 2560
COL_BQ, COL_BK, COL_BV = 3072, 3584, 4096
COL_MCKV = 4608
COL_KROPE = 4864
PROJ_COLS = 5120
SMALL_COLS = 128

TQ = 512
TK = 256
NQ = SEQ // TQ
NKV = SEQ // TK
INPROJ_BM, INPROJ_BN = 1024, 1024
ROW_TILE = 512
XATTN_ROWS = 1024
FFN_BM, FFN_BF = 1024, 256


def _cparams(sem):
    return pltpu.CompilerParams(dimension_semantics=sem,
                                vmem_limit_bytes=V7X_VMEM_LIMIT_BYTES)


class _Layer(NamedTuple):
    stacked: jax.Array
    index: int


def _whole(p):
    if isinstance(p, _Layer):
        shape = p.stacked.shape
        return pl.BlockSpec((None,) + shape[1:], lambda *_: (p.index,) + (0,) * (len(shape) - 1))
    return pl.BlockSpec(p.shape, lambda *_: (0,) * p.ndim)


def _arr(p):
    return p.stacked if isinstance(p, _Layer) else p


def _rms(x):
    return x * lax.rsqrt(jnp.mean(x * x, axis=-1, keepdims=True) + EPS)


def _dot(a, b):
    return jnp.dot(a, b, preferred_element_type=F32)


def _dot_nt(a, b, precision=None):
    return lax.dot_general(a, b, (((1,), (1,)), ((), ())),
                           preferred_element_type=F32, precision=precision)


VT_ROWS = HEAD_DIM + 16


def _flash_scratch():
    return [pltpu.VMEM((2, N_HEADS, TK, TQ), F32), pltpu.VMEM((2, N_HEADS, 1, TQ), F32),
            pltpu.VMEM((N_HEADS, 1, TQ), F32), pltpu.VMEM((N_HEADS, VT_ROWS, TQ), F32)]


def _tile_rows(j):
    return pl.ds(pl.multiple_of(j * TK, TK), TK)


def _flash_heads(i, q, kv, scratch, keep=None):
    s_ref, cm_ref, m_ref, acc_ref = scratch
    assert TQ == 2 * TK
    key_idx = lax.broadcasted_iota(jnp.int32, (TK, TK), 0)
    qry_idx = lax.broadcasted_iota(jnp.int32, (TK, TK), 1)
    causal = key_idx <= qry_idx

    def mask_overlap(h, j, s, second):
        left, right = s[:, :TK], s[:, TK:]
        if second:
            left = jnp.full_like(left, NEG)
            right = jnp.where(causal, right, NEG)
        else:
            left = jnp.where(causal, left, NEG)
            if keep is not None:
                right = jnp.where(keep(h, j)[:, TK:] > 0.5, right, NEG)
        return jnp.concatenate([left, right], axis=1)

    def stage_a(j, slot, overlap=None):
        raw = [_dot_nt(kv(h, j)[0], q[h]) for h in range(N_HEADS)]
        for h in range(N_HEADS):
            s, bias = raw[h], kv(h, j)[2]
            if bias is not None:
                s = s - bias
            if overlap is not None:
                s = mask_overlap(h, j, s, overlap)
            elif keep is not None:
                s = jnp.where(keep(h, j) > 0.5, s, NEG)
            s_ref[slot, h] = s
            cm_ref[slot, h] = jnp.max(s, axis=0, keepdims=True)

    def stage_b(j, slot):
        probs, alphas = [], []
        for h in range(N_HEADS):
            m_old = m_ref[h]
            m_new = jnp.maximum(m_old, cm_ref[slot, h])
            m_ref[h] = m_new
            probs.append(jnp.exp2(s_ref[slot, h] - m_new).astype(BF16))
            alphas.append(jnp.exp2(m_old - m_new))
        for h in range(N_HEADS):
            acc_ref[h] = alphas[h] * acc_ref[h] + _dot(kv(h, j)[1], probs[h])

    m_ref[...] = jnp.full(m_ref.shape, M_FLOOR, F32)
    acc_ref[...] = jnp.zeros(acc_ref.shape, F32)

    stage_a(2 * i, 0, overlap=False)
    stage_a(2 * i + 1, 1, overlap=True)
    stage_b(2 * i, 0)

    def body(p, _):
        pending = jnp.where(p == 0, 2 * i + 1, 2 * p - 1)
        stage_a(2 * p, 0)
        stage_b(pending, 1)
        stage_a(2 * p + 1, 1)
        stage_b(2 * p, 0)
        return 0

    lax.fori_loop(0, i, body, 0)
    stage_b(jnp.where(i == 0, 1, 2 * i - 1), 1)
    return [(acc_ref[h, :HEAD_DIM] * (1.0 / acc_ref[h, HEAD_DIM:HEAD_DIM + 1])).T for h in range(N_HEADS)]


def _store_vt(vt_ref, h, first_tile, v):
    base = h * VT_ROWS
    eye = (lax.broadcasted_iota(jnp.int32, (HEAD_DIM, HEAD_DIM), 0)
           == lax.broadcasted_iota(jnp.int32, (HEAD_DIM, HEAD_DIM), 1)).astype(BF16)
    for t in range(v.shape[0] // TK):
        _store_vt_tile(vt_ref, h, first_tile + t, _dot_nt(eye, v[t * TK:(t + 1) * TK]))


def _store_vt_tile(vt_ref, h, tile, vt):
    base = h * VT_ROWS
    vt_ref[tile, base:base + HEAD_DIM, :] = vt.astype(BF16)
    vt_ref[tile, base + HEAD_DIM:base + VT_ROWS, :] = jnp.ones((VT_ROWS - HEAD_DIM, TK), BF16)


def _group_out(heads, gn_ref, o_ref):
    o = jnp.concatenate(heads, axis=-1)
    o_ref[...] = (_rms(o) * gn_ref[...]).astype(o_ref.dtype)


def _inproj_kernel(x_ref, g_ref, w_ref, ws_ref, o_ref, os_ref, h_ref):
    @pl.when(pl.program_id(1) == 0)
    def _():
        h = (_rms(x_ref[...]) * g_ref[...]).astype(BF16)
        h_ref[...] = h
        os_ref[...] = _dot_nt(h, ws_ref[...])

    o_ref[...] = _dot_nt(h_ref[...], w_ref[...]).astype(o_ref.dtype)


def _inproj(x, g, wt, wst, l):
    bm, bn = INPROJ_BM, INPROJ_BN
    return pl.pallas_call(
        _inproj_kernel,
        grid=(TOKENS // bm, PROJ_COLS // bn),
        in_specs=[pl.BlockSpec((bm, D_MODEL), lambda m, n: (m, 0)),
                  _whole(g),
                  pl.BlockSpec((None, bn, D_MODEL), lambda m, n: (l, n, 0)),
                  pl.BlockSpec((None, SMALL_COLS, D_MODEL), lambda m, n: (l, 0, 0))],
        out_specs=[pl.BlockSpec((bm, bn), lambda m, n: (m, n)),
                   pl.BlockSpec((bm, SMALL_COLS), lambda m, n: (m, 0))],
        out_shape=[jax.ShapeDtypeStruct((TOKENS, PROJ_COLS), BF16),
                   jax.ShapeDtypeStruct((TOKENS, SMALL_COLS), F32)],
        scratch_shapes=[pltpu.VMEM((bm, D_MODEL), BF16)],
        compiler_params=_cparams(("arbitrary", "arbitrary")),
        name="inproj",
    )(x, _arr(g), wt, wst)


IN_SPLITS = (512, 512, 512, 4, 512, 256, 64, 512, 512, 512, 512, 512)
IN_COLS = sum(IN_SPLITS)
RELAYOUT_COLS = 256


def _rot_cols(w):
    half = MLA_ROPE_DIM // 2
    return jnp.concatenate([-w[..., half:], w[..., :half]], axis=-1)


def _relayout_kernel(w_ref, o_ref, s_ref):
    off = [0]
    for n in IN_SPLITS:
        off.append(off[-1] + n)
    half = MLA_ROPE_DIM // 2
    for l in range(DEPTH):
        w = w_ref[:, l, :]
        fq, fk, fv, ff, mcq, mckv, mkr, gu, gv, bq, bk, bv = (w[off[k]:off[k + 1]] for k in range(12))
        rot = jnp.concatenate([-mkr[half:], mkr[:half]], axis=0)
        o_ref[l] = jnp.concatenate([fq, fk, fv, mcq, gu, gv, bq, bk, bv, mckv, mkr, mkr, rot, rot],
                                   axis=0).astype(BF16)
        pad = jnp.zeros((SMALL_COLS - N_HEADS, w.shape[1]), F32)
        s_ref[l] = jnp.concatenate([ff, pad], axis=0).astype(BF16)


def _relayout_w_in(w_in):
    cols = RELAYOUT_COLS
    return pl.pallas_call(
        _relayout_kernel,
        grid=(D_MODEL // cols,),
        in_specs=[pl.BlockSpec((IN_COLS, DEPTH, cols), lambda m: (0, 0, m))],
        out_specs=[pl.BlockSpec((DEPTH, PROJ_COLS, cols), lambda m: (0, 0, m)),
                   pl.BlockSpec((DEPTH, SMALL_COLS, cols), lambda m: (0, 0, m))],
        out_shape=[jax.ShapeDtypeStruct((DEPTH, PROJ_COLS, D_MODEL), BF16),
                   jax.ShapeDtypeStruct((DEPTH, SMALL_COLS, D_MODEL), BF16)],
        compiler_params=_cparams(("arbitrary",)),
        name="relayout_w_in",
    )(jnp.transpose(w_in, (2, 0, 1)))


def _fox_kernel(q_ref, k_ref, v_ref, f_ref, bf_ref, qg_ref, kg_ref, gn_ref, o_ref,
                kn_ref, vt_ref, c_ref, *flash):
    i = pl.program_id(1)
    scale = HEAD_DIM ** -0.5 * LOG2E
    sls = [slice(h * HEAD_DIM, (h + 1) * HEAD_DIM) for h in range(N_HEADS)]

    @pl.when(i == 0)
    def _prep():
        x = f_ref[...] + bf_ref[...]
        c = jnp.minimum(x, 0.0) - jnp.log1p(jnp.exp(-jnp.abs(x)))
        row = lax.broadcasted_iota(jnp.int32, c.shape, 0)
        step = 1
        while step < SEQ:
            c = c + jnp.where(row >= step, pltpu.roll(c, step, 0), 0.0)
            step *= 2
        for h in range(N_HEADS):
            kn_ref[:, sls[h]] = (_rms(k_ref[:, sls[h]].astype(F32)) * kg_ref[...]).astype(BF16)
            _store_vt(vt_ref, h, 0, v_ref[:, sls[h]])
            c_ref[h] = jnp.broadcast_to(c[:, h:h + 1] * LOG2E, (SEQ, HEAD_DIM))

    q = [(_rms(q_ref[:, sl].astype(F32)) * qg_ref[...] * scale).astype(BF16) for sl in sls]

    def kv(h, j):
        c = c_ref[h, _tile_rows(j), :]
        return (kn_ref[_tile_rows(j), sls[h]],
                vt_ref[j, h * VT_ROWS:(h + 1) * VT_ROWS, :],
                jnp.concatenate([c] * (TQ // HEAD_DIM), axis=1))

    _group_out(_flash_heads(i, q, kv, flash), gn_ref, o_ref)


def _fox(proj, small, b_f, qg, kg, gn):
    return pl.pallas_call(
        _fox_kernel,
        grid=(BATCH, NQ),
        in_specs=[pl.BlockSpec((TQ, GROUP_WIDTH), lambda b, i: (b * NQ + i, COL_FQ // GROUP_WIDTH)),
                  pl.BlockSpec((SEQ, GROUP_WIDTH), lambda b, i: (b, COL_FK // GROUP_WIDTH)),
                  pl.BlockSpec((SEQ, GROUP_WIDTH), lambda b, i: (b, COL_FV // GROUP_WIDTH)),
                  pl.BlockSpec((SEQ, SMALL_COLS), lambda b, i: (b, 0)),
                  _whole(b_f), _whole(qg), _whole(kg), _whole(gn)],
        out_specs=pl.BlockSpec((TQ, GROUP_WIDTH), lambda b, i: (b * NQ + i, 0)),
        out_shape=jax.ShapeDtypeStruct((TOKENS, GROUP_WIDTH), BF16),
        scratch_shapes=[pltpu.VMEM((SEQ, GROUP_WIDTH), BF16),
                        pltpu.VMEM((NKV, N_HEADS * VT_ROWS, TK), BF16),
                        pltpu.VMEM((N_HEADS, SEQ, HEAD_DIM), F32)] + _flash_scratch(),
        compiler_params=_cparams(("arbitrary", "arbitrary")),
        name="fox",
    )(proj, proj, proj, small, _arr(b_f), _arr(qg), _arr(kg), _arr(gn))


MLA_Q_HEAD_COLS = 384
MLA_K_COLS = 256


def _rope128(a, b, ga, gb, cos, sin):
    r = lax.rsqrt(jnp.sum(a * a, axis=-1, keepdims=True) * (1.0 / (2 * MLA_ROPE_DIM)) + EPS)
    return r * (a * ga * cos + b * gb * sin)


def _mla_kernel(cq_ref, ckv_ref, kr_ref, cos_ref, sin_ref, cosq_ref, sinq_ref,
                qlg_ref, wuq_ref, kvlg_ref, wuk_ref, wuvt_ref, qgn_ref, qga_ref, qgb_ref,
                kgn_ref, kga_ref, kgb_ref, gn_ref, o_ref, kf_ref, vt_ref, *flash):
    i = pl.program_id(1)
    scale = MLA_QK_DIM ** -0.5 * LOG2E
    chunk = 512

    @pl.when(i == 0)
    def _prep():
        for c in range(SEQ // chunk):
            rows = slice(c * chunk, (c + 1) * chunk)
            ckv = (_rms(ckv_ref[rows, :].astype(F32)) * kvlg_ref[...]).astype(BF16)
            kn = _dot(ckv, wuk_ref[...])
            kr = kr_ref[rows, :].astype(F32)
            kpe = _rope128(kr[:, :128], kr[:, 128:], kga_ref[...], kgb_ref[...],
                           cos_ref[rows, :], sin_ref[rows, :]).astype(BF16)
            for h in range(N_HEADS):
                sl = slice(h * HEAD_DIM, (h + 1) * HEAD_DIM)
                kf_ref[h, rows, 0:128] = (_rms(kn[:, sl]) * kgn_ref[...]).astype(BF16)
                kf_ref[h, rows, 128:256] = kpe
                vt = _dot_nt(wuvt_ref[sl, :], ckv)
                for t in range(chunk // TK):
                    _store_vt_tile(vt_ref, h, c * (chunk // TK) + t, vt[:, t * TK:(t + 1) * TK])

    cq = (_rms(cq_ref[...].astype(F32)) * qlg_ref[...]).astype(BF16)
    qraw = _dot(cq, wuq_ref[...])
    qf = []
    for h in range(N_HEADS):
        base = h * MLA_Q_HEAD_COLS
        qn = _rms(qraw[:, base:base + 128]) * qgn_ref[...]
        qpe = _rope128(qraw[:, base + 128:base + 256], qraw[:, base + 256:base + 384],
                       qga_ref[...], qgb_ref[...], cosq_ref[...], sinq_ref[...])
        qf.append((jnp.concatenate([qn, qpe], axis=-1) * scale).astype(BF16))

    def kv(h, j):
        return kf_ref[h, _tile_rows(j), :], vt_ref[j, h * VT_ROWS:(h + 1) * VT_ROWS, :], None

    _group_out(_flash_heads(i, qf, kv, flash), gn_ref, o_ref)


def _mla(proj, cos, sin, qlg, wuq, kvlg, wuk, wuvt, qgn, qga, qgb, kgn, kga, kgb, gn):
    full = _whole
    consts = (qlg, wuq, kvlg, wuk, wuvt, qgn, qga, qgb, kgn, kga, kgb, gn)
    return pl.pallas_call(
        _mla_kernel,
        grid=(BATCH, NQ),
        in_specs=[pl.BlockSpec((TQ, MLA_Q_RANK), lambda b, i: (b * NQ + i, COL_MCQ // MLA_Q_RANK)),
                  pl.BlockSpec((SEQ, MLA_KV_RANK), lambda b, i: (b, COL_MCKV // MLA_KV_RANK)),
                  pl.BlockSpec((SEQ, 256), lambda b, i: (b, COL_KROPE // 256)),
                  full(cos), full(sin),
                  pl.BlockSpec((TQ, 128), lambda b, i: (i, 0)),
                  pl.BlockSpec((TQ, 128), lambda b, i: (i, 0)),
                  *[full(c) for c in consts]],
        out_specs=pl.BlockSpec((TQ, GROUP_WIDTH), lambda b, i: (b * NQ + i, 0)),
        out_shape=jax.ShapeDtypeStruct((TOKENS, GROUP_WIDTH), BF16),
        scratch_shapes=[pltpu.VMEM((N_HEADS, SEQ, MLA_K_COLS), BF16),
                        pltpu.VMEM((NKV, N_HEADS * VT_ROWS, TK), BF16)] + _flash_scratch(),
        compiler_params=_cparams(("arbitrary", "arbitrary")),
        name="mla",
    )(proj, proj, proj, cos, sin, cos, sin, *[_arr(c) for c in consts])


def _gelu(x):
    return 0.5 * x * (1.0 + jnp.tanh(0.7978845608028654 * (x + 0.044715 * (x * x * x))))


def _gmlp_kernel(u_ref, v_ref, lg_ref, lb_ref, ws_ref, bs_ref, gn_ref, o_ref, mix_ref):
    v = _gelu(v_ref[...].astype(F32))
    mu = jnp.mean(v, axis=-1, keepdims=True)
    vc = v - mu
    var = jnp.mean(vc * vc, axis=-1, keepdims=True)
    vn = (vc * lax.rsqrt(var + EPS) * lg_ref[...] + lb_ref[...]).astype(BF16)
    r = lax.broadcasted_iota(jnp.int32, (GMLP_CHUNK, GMLP_CHUNK), 0)
    c = lax.broadcasted_iota(jnp.int32, (GMLP_CHUNK, GMLP_CHUNK), 1)
    for g in range(GROUP_WIDTH // HEAD_DIM):
        w = jnp.where(c <= r, ws_ref[g], 0.0).astype(BF16)
        cols = slice(g * HEAD_DIM, (g + 1) * HEAD_DIM)
        for n in range(ROW_TILE // GMLP_CHUNK):
            rows = slice(n * GMLP_CHUNK, (n + 1) * GMLP_CHUNK)
            mix_ref[rows, cols] = _dot(w, vn[rows, cols]) + bs_ref[:, cols]
    o = _gelu(u_ref[...].astype(F32)) * mix_ref[...]
    o_ref[...] = (_rms(o) * gn_ref[...]).astype(o_ref.dtype)


def _gmlp(proj, lg, lb, ws, bs, gn):
    consts = (lg, lb, ws, bs, gn)
    return pl.pallas_call(
        _gmlp_kernel,
        grid=(TOKENS // ROW_TILE,),
        in_specs=[pl.BlockSpec((ROW_TILE, GROUP_WIDTH), lambda m: (m, COL_GU // GROUP_WIDTH)),
                  pl.BlockSpec((ROW_TILE, GROUP_WIDTH), lambda m: (m, COL_GV // GROUP_WIDTH)),
                  *[_whole(c) for c in consts]],
        out_specs=pl.BlockSpec((ROW_TILE, GROUP_WIDTH), lambda m: (m, 0)),
        out_shape=jax.ShapeDtypeStruct((TOKENS, GROUP_WIDTH), BF16),
        scratch_shapes=[pltpu.VMEM((ROW_TILE, GROUP_WIDTH), F32)],
        compiler_params=_cparams(("arbitrary",)),
        name="gmlp",
    )(proj, proj, *[_arr(c) for c in consts])


def _moba_kernel(q_ref, k_ref, v_ref, qg_ref, kg_ref, gn_ref, o_ref, kn_ref, vt_ref, km_ref,
                 sel_ref, *flash):
    i = pl.program_id(1)
    scale = HEAD_DIM ** -0.5 * LOG2E

    @pl.when(i == 0)
    def _prep():
        for h in range(N_HEADS):
            sl = slice(h * HEAD_DIM, (h + 1) * HEAD_DIM)
            means = []
            for n in range(N_MOBA_BLOCKS):
                rows = slice(n * MOBA_BLOCK, (n + 1) * MOBA_BLOCK)
                kn = _rms(k_ref[rows, sl].astype(F32)) * kg_ref[...]
                kn_ref[rows, sl] = kn.astype(BF16)
                means.append(jnp.mean(kn, axis=0, keepdims=True))
            km_ref[h] = jnp.concatenate(means, axis=0)
            _store_vt(vt_ref, h, 0, v_ref[:, sl])

    blk = lax.broadcasted_iota(jnp.int32, (N_MOBA_BLOCKS, TQ), 0)
    second = lax.broadcasted_iota(jnp.int32, (1, TQ), 1) >= MOBA_BLOCK
    own = 2 * i + jnp.where(second, 1, 0)
    sls = [slice(h * HEAD_DIM, (h + 1) * HEAD_DIM) for h in range(N_HEADS)]
    qn = []
    for h in range(N_HEADS):
        q32 = _rms(q_ref[:, sls[h]].astype(F32)) * qg_ref[...]
        qn.append((q32 * scale).astype(BF16))

        gate = _dot_nt(km_ref[h], q32, precision=lax.Precision.HIGHEST)
        gate = jnp.where(blk < own, gate, NEG)
        sel_t = jnp.zeros((N_MOBA_BLOCKS, TQ), F32)
        for n in range(N_MOBA_BLOCKS):
            gn_row = gate[n:n + 1, :]
            beats = jnp.where(gate > gn_row, 1.0, jnp.where((gate == gn_row) & (blk < n), 1.0, 0.0))
            rank = jnp.sum(beats, axis=0, keepdims=True)
            past = n < own
            chosen = jnp.where(rank < MOBA_TOPK, jnp.where(past, 1.0, 0.0), 0.0)
            sel_t = jnp.where(blk == n, chosen, sel_t)
        sel_ref[h] = sel_t

    def kv(h, j):
        return kn_ref[_tile_rows(j), sls[h]], vt_ref[j, h * VT_ROWS:(h + 1) * VT_ROWS, :], None

    def keep(h, j):
        return sel_ref[h, pl.ds(j, 1), :]

    _group_out(_flash_heads(i, qn, kv, flash, keep), gn_ref, o_ref)


def _moba(proj, qg, kg, gn):
    return pl.pallas_call(
        _moba_kernel,
        grid=(BATCH, NQ),
        in_specs=[pl.BlockSpec((TQ, GROUP_WIDTH), lambda b, i: (b * NQ + i, COL_BQ // GROUP_WIDTH)),
                  pl.BlockSpec((SEQ, GROUP_WIDTH), lambda b, i: (b, COL_BK // GROUP_WIDTH)),
                  pl.BlockSpec((SEQ, GROUP_WIDTH), lambda b, i: (b, COL_BV // GROUP_WIDTH)),
                  _whole(qg), _whole(kg), _whole(gn)],
        out_specs=pl.BlockSpec((TQ, GROUP_WIDTH), lambda b, i: (b * NQ + i, 0)),
        out_shape=jax.ShapeDtypeStruct((TOKENS, GROUP_WIDTH), BF16),
        scratch_shapes=[pltpu.VMEM((SEQ, GROUP_WIDTH), BF16),
                        pltpu.VMEM((NKV, N_HEADS * VT_ROWS, TK), BF16),
                        pltpu.VMEM((N_HEADS, N_MOBA_BLOCKS, HEAD_DIM), F32),
                        pltpu.VMEM((N_HEADS, N_MOBA_BLOCKS, TQ), F32)] + _flash_scratch(),
        compiler_params=_cparams(("arbitrary", "arbitrary")),
        name="moba",
    )(proj, proj, proj, _arr(qg), _arr(kg), _arr(gn))


def _outproj_kernel(x_ref, a_ref, b_ref, c_ref, d_ref, w_ref, o_ref):
    mixed = jnp.concatenate([a_ref[...], b_ref[...], c_ref[...], d_ref[...]], axis=-1)
    o_ref[...] = x_ref[...] + _dot(mixed, w_ref[...])


def _outproj(x, oa, ob, oc, od, w):
    grp = pl.BlockSpec((ROW_TILE, GROUP_WIDTH), lambda m: (m, 0))
    return pl.pallas_call(
        _outproj_kernel,
        grid=(TOKENS // ROW_TILE,),
        in_specs=[pl.BlockSpec((ROW_TILE, D_MODEL), lambda m: (m, 0)), grp, grp, grp, grp, _whole(w)],
        out_specs=pl.BlockSpec((ROW_TILE, D_MODEL), lambda m: (m, 0)),
        out_shape=jax.ShapeDtypeStruct((TOKENS, D_MODEL), F32),
        compiler_params=_cparams(("arbitrary",)),
        name="outproj",
    )(x, oa, ob, oc, od, _arr(w))


def _memkv_kernel(m_ref, g_ref, w_ref, kg_ref, k_ref, v_ref):
    h = (_rms(m_ref[...]) * g_ref[...]).astype(BF16)
    kv = _dot(h, w_ref[...])
    for hd in range(N_HEADS):
        sl = slice(hd * HEAD_DIM, (hd + 1) * HEAD_DIM)
        k_ref[:, sl] = (_rms(kv[:, sl]) * kg_ref[...]).astype(BF16)
    v_ref[...] = kv[:, GROUP_WIDTH:].astype(BF16)


def _memkv(mem, g, w, kg):
    out = pl.BlockSpec((MEM_LEN, GROUP_WIDTH), lambda b: (b, 0))
    return pl.pallas_call(
        _memkv_kernel,
        grid=(BATCH,),
        in_specs=[pl.BlockSpec((MEM_LEN, D_MODEL), lambda b: (b, 0)),
                  _whole(g), _whole(w), _whole(kg)],
        out_specs=[out, out],
        out_shape=[jax.ShapeDtypeStruct((BATCH * MEM_LEN, GROUP_WIDTH), BF16)] * 2,
        compiler_params=_cparams(("arbitrary",)),
        name="memkv",
    )(mem, _arr(g), _arr(w), _arr(kg))


def _xattn_kernel(x_ref, g_ref, wq_ref, qg_ref, k_ref, v_ref, wo_ref, o_ref):
    scale = HEAD_DIM ** -0.5 * LOG2E
    for r in range(XATTN_ROWS // ROW_TILE):
        rows = slice(r * ROW_TILE, (r + 1) * ROW_TILE)
        x = x_ref[rows, :]
        h = (_rms(x) * g_ref[...]).astype(BF16)
        q = _dot(h, wq_ref[...])
        heads = []
        for hd in range(N_HEADS):
            sl = slice(hd * HEAD_DIM, (hd + 1) * HEAD_DIM)
            qn = (_rms(q[:, sl]) * qg_ref[...] * scale).astype(BF16)
            s = _dot_nt(qn, k_ref[:, sl])
            p = jnp.exp2(s - jnp.max(s, axis=-1, keepdims=True))
            o = _dot(p.astype(BF16), v_ref[:, sl])
            heads.append(o / jnp.sum(p, axis=-1, keepdims=True))
        a = jnp.concatenate(heads, axis=-1).astype(BF16)
        o_ref[rows, :] = x + _dot(a, wo_ref[...])


def _xattn(x, g, wq, qg, k, v, wo):
    per_seq = SEQ // XATTN_ROWS
    return pl.pallas_call(
        _xattn_kernel,
        grid=(TOKENS // XATTN_ROWS,),
        in_specs=[pl.BlockSpec((XATTN_ROWS, D_MODEL), lambda m: (m, 0)),
                  _whole(g), _whole(wq), _whole(qg),
                  pl.BlockSpec((MEM_LEN, GROUP_WIDTH), lambda m: (m // per_seq, 0)),
                  pl.BlockSpec((MEM_LEN, GROUP_WIDTH), lambda m: (m // per_seq, 0)),
                  _whole(wo)],
        out_specs=pl.BlockSpec((XATTN_ROWS, D_MODEL), lambda m: (m, 0)),
        out_shape=jax.ShapeDtypeStruct((TOKENS, D_MODEL), F32),
        compiler_params=_cparams(("arbitrary",)),
        name="xattn",
    )(x, _arr(g), _arr(wq), _arr(qg), k, v, _arr(wo))


def _ffn_kernel(x_ref, g_ref, wg_ref, wu_ref, wd_ref, o_ref, h_ref):
    @pl.when(pl.program_id(1) == 0)
    def _():
        x = x_ref[...]
        h_ref[...] = (_rms(x) * g_ref[...]).astype(BF16)
        o_ref[...] = x

    h = h_ref[...]
    gate = _dot(h, wg_ref[...].astype(BF16))
    up = _dot(h, wu_ref[...].astype(BF16))
    a = (gate * (1.0 / (1.0 + jnp.exp(-gate))) * up).astype(BF16)
    o_ref[...] += _dot(a, wd_ref[...].astype(BF16))


def _ffn(x, g, wgu, wd, l):
    bm, bf = FFN_BM, FFN_BF
    nf = D_FF // bf
    return pl.pallas_call(
        _ffn_kernel,
        grid=(TOKENS // bm, nf),
        in_specs=[pl.BlockSpec((bm, D_MODEL), lambda m, f: (m, 0)),
                  _whole(g),
                  pl.BlockSpec((None, D_MODEL, bf), lambda m, f: (l, 0, f)),
                  pl.BlockSpec((None, D_MODEL, bf), lambda m, f: (l, 0, nf + f)),
                  pl.BlockSpec((None, bf, D_MODEL), lambda m, f: (l, f, 0))],
        out_specs=pl.BlockSpec((bm, D_MODEL), lambda m, f: (m, 0)),
        out_shape=jax.ShapeDtypeStruct((TOKENS, D_MODEL), F32),
        scratch_shapes=[pltpu.VMEM((bm, D_MODEL), BF16)],
        compiler_params=_cparams(("arbitrary", "arbitrary")),
        name="ffn",
    )(x, _arr(g), wgu, wgu, wd)


def _swap_halves(g):
    half = MLA_ROPE_DIM // 2
    return jnp.concatenate([g[..., half:], g[..., :half]], axis=-1)


def _rope_tables():
    pos = jnp.arange(SEQ, dtype=F32)
    inv_freq = ROPE_THETA ** (-jnp.arange(0, MLA_ROPE_DIM, 2, dtype=F32) / MLA_ROPE_DIM)
    ang = pos[:, None] * inv_freq[None, :]
    zeros = jnp.zeros((SEQ, MLA_ROPE_DIM), F32)
    cos = jnp.concatenate([jnp.cos(ang), jnp.cos(ang), zeros], axis=-1)
    sin = jnp.concatenate([jnp.sin(ang), jnp.sin(ang), zeros], axis=-1)
    return cos, sin


def _rows(v):
    return v.reshape(-1, 1, v.shape[-1]).astype(F32)


def _stacked_params(p):
    wuq = p['mla_w_uq'].astype(BF16).reshape(DEPTH, MLA_Q_RANK, N_HEADS, MLA_QK_DIM)
    nope, pe = wuq[..., :MLA_NOPE_DIM], wuq[..., MLA_NOPE_DIM:]
    rot = _rot_cols(pe)
    wuq = jnp.concatenate([nope, pe, pe, rot, rot], axis=-1).reshape(
        DEPTH, MLA_Q_RANK, N_HEADS * MLA_Q_HEAD_COLS)
    wukv = p['mla_w_ukv'].astype(BF16).reshape(DEPTH, MLA_KV_RANK, N_HEADS, 2 * HEAD_DIM)
    wuk = wukv[..., :HEAD_DIM].reshape(DEPTH, MLA_KV_RANK, GROUP_WIDTH)
    wuvt = jnp.swapaxes(wukv[..., HEAD_DIM:].reshape(DEPTH, MLA_KV_RANK, GROUP_WIDTH), 1, 2)
    qg, kg = p['mla_q_norm'], p['mla_k_norm']
    two = lambda g: _rows(jnp.concatenate([g, g], axis=-1))
    return dict(
        mix_norm=_rows(p['mix_norm']),
        fox_b_f=_rows(jnp.pad(p['fox_b_f'], ((0, 0), (0, SMALL_COLS - N_HEADS)))),
        fox_q=_rows(p['fox_q_norm']), fox_k=_rows(p['fox_k_norm']),
        mla_qlg=_rows(p['mla_q_lora_norm']), mla_wuq=wuq,
        mla_kvlg=_rows(p['mla_kv_lora_norm']), mla_wuk=wuk, mla_wuvt=wuvt,
        mla_qgn=_rows(qg[:, :MLA_NOPE_DIM]), mla_qga=two(qg[:, MLA_NOPE_DIM:]),
        mla_qgb=two(_swap_halves(qg[:, MLA_NOPE_DIM:])),
        mla_kgn=_rows(kg[:, :MLA_NOPE_DIM]), mla_kga=two(kg[:, MLA_NOPE_DIM:]),
        mla_kgb=two(_swap_halves(kg[:, MLA_NOPE_DIM:])),
        gmlp_lg=_rows(p['gmlp_ln_g']), gmlp_lb=_rows(p['gmlp_ln_b']),
        gmlp_ws=p['gmlp_w_s'].astype(F32),
        gmlp_bs=jnp.repeat(jnp.swapaxes(p['gmlp_b_s'], 1, 2), HEAD_DIM, axis=2).astype(F32),
        moba_q=_rows(p['moba_q_norm']), moba_k=_rows(p['moba_k_norm']),
        gn=_rows(p['group_norm']),
        w_out=p['w_out'].astype(BF16),
        xattn_norm=_rows(p['xattn_norm']), mem_norm=_rows(p['mem_norm']),
        w_xq=p['w_xq'].astype(BF16), w_xkv=p['w_xkv'].astype(BF16),
        xq=_rows(p['xattn_q_norm']), xk=_rows(p['xattn_k_norm']),
        w_xo=p['w_xo'].astype(BF16),
        ffn_norm=_rows(p['ffn_norm']),
    )


def kernel(x, mem, mix_norm, w_in, fox_b_f, fox_q_norm, fox_k_norm, mla_q_lora_norm, mla_w_uq,
           mla_kv_lora_norm, mla_w_ukv, mla_q_norm, mla_k_norm, gmlp_ln_g, gmlp_ln_b, gmlp_w_s,
           gmlp_b_s, moba_q_norm, moba_k_norm, group_norm, w_out, xattn_norm, mem_norm, w_xq, w_xkv,
           xattn_q_norm, xattn_k_norm, w_xo, ffn_norm, w_gate_up, w_down):
    params = dict(mix_norm=mix_norm, w_in=w_in, fox_b_f=fox_b_f, fox_q_norm=fox_q_norm,
                  fox_k_norm=fox_k_norm, mla_q_lora_norm=mla_q_lora_norm, mla_w_uq=mla_w_uq,
                  mla_kv_lora_norm=mla_kv_lora_norm, mla_w_ukv=mla_w_ukv, mla_q_norm=mla_q_norm,
                  mla_k_norm=mla_k_norm, gmlp_ln_g=gmlp_ln_g, gmlp_ln_b=gmlp_ln_b,
                  gmlp_w_s=gmlp_w_s, gmlp_b_s=gmlp_b_s, moba_q_norm=moba_q_norm,
                  moba_k_norm=moba_k_norm, group_norm=group_norm, w_out=w_out,
                  xattn_norm=xattn_norm, mem_norm=mem_norm, w_xq=w_xq, w_xkv=w_xkv,
                  xattn_q_norm=xattn_q_norm, xattn_k_norm=xattn_k_norm, w_xo=w_xo,
                  ffn_norm=ffn_norm, w_gate_up=w_gate_up, w_down=w_down)
    st = _stacked_params(params)
    w_main_t, w_small_t = _relayout_w_in(w_in)
    cos, sin = _rope_tables()
    xf = x.reshape(TOKENS, D_MODEL)
    memf = mem.reshape(BATCH * MEM_LEN, D_MODEL)
    for l in range(DEPTH):
        q = {name: _Layer(a, l) for name, a in st.items()}
        gn = [_Layer(st['gn'], l * 4 + g) for g in range(4)]
        proj, small = _inproj(xf, q['mix_norm'], w_main_t, w_small_t, l)
        o_a = _fox(proj, small, q['fox_b_f'], q['fox_q'], q['fox_k'], gn[0])
        o_b = _mla(proj, cos, sin, q['mla_qlg'], q['mla_wuq'], q['mla_kvlg'], q['mla_wuk'], q['mla_wuvt'],
                   q['mla_qgn'], q['mla_qga'], q['mla_qgb'], q['mla_kgn'], q['mla_kga'],
                   q['mla_kgb'], gn[1])
        o_c = _gmlp(proj, q['gmlp_lg'], q['gmlp_lb'], q['gmlp_ws'], q['gmlp_bs'], gn[2])
        o_d = _moba(proj, q['moba_q'], q['moba_k'], gn[3])
        xf = _outproj(xf, o_a, o_b, o_c, o_d, q['w_out'])
        mk, mv = _memkv(memf, q['mem_norm'], q['w_xkv'], q['xk'])
        xf = _xattn(xf, q['xattn_norm'], q['w_xq'], q['xq'], mk, mv, q['w_xo'])
        xf = _ffn(xf, q['ffn_norm'], w_gate_up, w_down, l)
    return xf.reshape(BATCH, SEQ, D_MODEL)
```
